```python
import jax, jax.numpy as jnp
from jax import lax
import numpy as np

D_MODEL = 1024
BATCH = 2
SEQ = 16384
DEPTH = 1

CHUNK = 64
EPS = 1e-6
GDN_HEADS = 4
GDN_DK = 128
GDN_DV = 128
GDN_CONV = 4
ATT_HEADS = 8
ATT_DH = 64
ATT_BAND = 9
REL_CLIP = 128
D_FF = 2816
FFN_CONV = 3

KEY_A = GDN_HEADS * GDN_DK
VAL_A = GDN_HEADS * GDN_DV
WIDTH_B = ATT_HEADS * ATT_DH
IN_SIZES = (KEY_A, KEY_A, VAL_A, VAL_A, GDN_HEADS, GDN_HEADS,
            WIDTH_B, WIDTH_B, WIDTH_B, D_MODEL, D_MODEL)
IN_SPLITS = tuple(sum(IN_SIZES[:i + 1]) for i in range(len(IN_SIZES) - 1))
D_IN = sum(IN_SIZES)
CONV_A = 2 * KEY_A + VAL_A

kernel_name = "hybrid_gdn_bandattn_convffn"


def rmsnorm(x, w):
    xf = x.astype(jnp.float32)
    y = xf * lax.rsqrt(jnp.mean(xf * xf, axis=-1, keepdims=True) + EPS)
    return (y * w.astype(jnp.float32)).astype(x.dtype)


def l2norm(x):
    xf = x.astype(jnp.float32)
    return xf * lax.rsqrt(jnp.sum(xf * xf, axis=-1, keepdims=True) + EPS)


def causal_dwconv(x, w):
    width = w.shape[0]
    return lax.conv_general_dilated(
        x, w[:, None, :].astype(x.dtype), window_strides=(1,), padding=[(width - 1, 0)],
        dimension_numbers=('NWC', 'WIO', 'NWC'), feature_group_count=x.shape[-1])


def gated_delta_rule(q, k, v, g, beta):
    b_, t_, h_, dk = q.shape
    dv = v.shape[-1]
    n = t_ // CHUNK

    def to_chunks(a):
        return a.astype(jnp.float32).reshape(b_, n, CHUNK, h_, -1).transpose(1, 0, 3, 2, 4)

    q = to_chunks(q) * (dk ** -0.5)
    k = to_chunks(k)
    v = to_chunks(v)
    g = g.astype(jnp.float32).reshape(b_, n, CHUNK, h_).transpose(1, 0, 3, 2)
    beta = beta.astype(jnp.float32).reshape(b_, n, CHUNK, h_).transpose(1, 0, 3, 2)

    G = jnp.cumsum(g, axis=-1)
    idx = jnp.arange(CHUNK)
    strict = idx[:, None] > idx[None, :]
    incl = idx[:, None] >= idx[None, :]
    diff = G[..., :, None] - G[..., None, :]
    dec_strict = jnp.exp(jnp.where(strict, diff, -jnp.inf))
    dec_incl = jnp.exp(jnp.where(incl, diff, -jnp.inf))
    gam = jnp.exp(G)

    a_mat = beta[..., :, None] * jnp.einsum('nbhid,nbhjd->nbhij', k, k) * dec_strict
    eye = jnp.eye(CHUNK, dtype=jnp.float32)
    rhs = jnp.concatenate([(beta * gam)[..., None] * k, beta[..., None] * v], axis=-1)
    sol = lax.linalg.triangular_solve(a_mat + eye, rhs, left_side=True, lower=True)
    w_c = sol[..., :dk]
    uv_c = sol[..., dk:]
    p_c = jnp.einsum('nbhid,nbhjd->nbhij', q, k) * dec_incl
    qg_c = q * gam[..., None]
    kd_c = k * jnp.exp(G[..., -1:] - G)[..., None]
    gl_c = gam[..., -1]

    def step(s, xs):
        w_i, uv_i, p_i, qg_i, kd_i, gl_i = xs
        u = uv_i - jnp.einsum('bhid,bhde->bhie', w_i, s)
        o = jnp.einsum('bhid,bhde->bhie', qg_i, s) + jnp.einsum('bhij,bhje->bhie', p_i, u)
        s = gl_i[..., None, None] * s + jnp.einsum('bhid,bhie->bhde', kd_i, u)
        return s, o

    s0 = jnp.zeros((b_, h_, dk, dv), jnp.float32)
    _, o = lax.scan(step, s0, (w_c, uv_c, p_c, qg_c, kd_c, gl_c))
    return o.transpose(1, 0, 3, 2, 4).reshape(b_, t_, h_, dv)


def chunk_band_attention(q, k, v, rel_table):
    b_, t_, h_, dh = q.shape
    n = t_ // CHUNK
    band = ATT_BAND * CHUNK
    lead = (ATT_BAND - 1) * CHUNK
    kp = jnp.pad(k, ((0, 0), (lead, 0), (0, 0), (0, 0)))
    vp = jnp.pad(v, ((0, 0), (lead, 0), (0, 0), (0, 0)))
    r = jnp.arange(CHUNK)
    j = jnp.arange(band)
    dist = lead + r[:, None] - j[None, :]
    bias = rel_table.astype(jnp.float32)[:, jnp.clip(dist, -REL_CLIP, REL_CLIP) + REL_CLIP]
    scale = dh ** -0.5

    def one_chunk(c):
        start = c * CHUNK
        qc = lax.dynamic_slice_in_dim(q, start, CHUNK, axis=1)
        kc = lax.dynamic_slice_in_dim(kp, start, band, axis=1)
        vc = lax.dynamic_slice_in_dim(vp, start, band, axis=1)
        s = jnp.einsum('bqhd,bkhd->bhqk', qc, kc).astype(jnp.float32) * scale + bias
        valid = j >= lead - start
        s = jnp.where(valid[None, None, None, :], s, -jnp.inf)
        p = jax.nn.softmax(s, axis=-1).astype(v.dtype)
        return jnp.einsum('bhqk,bkhd->bqhd', p, vc)

    out = lax.map(one_chunk, jnp.arange(n))
    return out.transpose(1, 0, 2, 3, 4).reshape(b_, t_, h_ * dh)


def setup_inputs(seed: int = 0) -> dict:
    key = jax.random.key(seed)
    ks = jax.random.split(key, 20)
    f32 = jnp.float32
    nrm = lambda k, shape, s: jax.random.normal(k, shape, f32) * s
    gain = lambda k, shape: 1.0 + 0.02 * jax.random.normal(k, shape, f32)
    dt = jnp.exp(jax.random.uniform(ks[4], (DEPTH, GDN_HEADS), f32, np.log(1e-3), np.log(1e-1)))
    return {
        "x": nrm(ks[0], (BATCH, SEQ, D_MODEL), 1.0),
        "norm_mix_w": gain(ks[1], (DEPTH, D_MODEL)),
        "w_in": nrm(ks[2], (DEPTH, D_MODEL, D_IN), D_MODEL ** -0.5),
        "conv_qkv_w": nrm(ks[3], (DEPTH, GDN_CONV, CONV_A), GDN_CONV ** -0.5),
        "a_log": jnp.log(jax.random.uniform(ks[5], (DEPTH, GDN_HEADS), f32, 1.0, 16.0)),
        "dt_bias": dt + jnp.log(-jnp.expm1(-dt)),
        "gdn_norm_w": gain(ks[6], (DEPTH, GDN_DV)),
        "w_branch_a": nrm(ks[7], (DEPTH, VAL_A, D_MODEL), VAL_A ** -0.5),
        "w_branch_b": nrm(ks[8], (DEPTH, WIDTH_B, D_MODEL), WIDTH_B ** -0.5),
        "rel_bias": nrm(ks[9], (DEPTH, ATT_HEADS, 2 * REL_CLIP + 1), 0.5),
        "w_out": nrm(ks[10], (DEPTH, D_MODEL, D_MODEL), D_MODEL ** -0.5),
        "norm_ffn_w": gain(ks[11], (DEPTH, D_MODEL)),
        "w_up": nrm(ks[12], (DEPTH, D_MODEL, 2 * D_FF), D_MODEL ** -0.5),
        "conv_ffn_w": nrm(ks[13], (DEPTH, FFN_CONV, 2 * D_FF), FFN_CONV ** -0.5),
        "conv_ffn_b": nrm(ks[14], (DEPTH, 2 * D_FF), 0.02),
        "w_down": nrm(ks[15], (DEPTH, D_FF, D_MODEL), D_FF ** -0.5),
        "norm_final_w": gain(ks[16], (D_MODEL,)),
    }


def reference(x, norm_mix_w, w_in, conv_qkv_w, a_log, dt_bias, gdn_norm_w, w_branch_a, w_branch_b,
              rel_bias, w_out, norm_ffn_w, w_up, conv_ffn_w, conv_ffn_b, w_down, norm_final_w):
    b_, t_, _ = x.shape
    for l in range(DEPTH):
        h = rmsnorm(x, norm_mix_w[l])
        proj = h @ w_in[l]
        qa, ka, va, za, ba, aa, qb, kb, vb, ga, gb = jnp.split(proj, IN_SPLITS, axis=-1)

        qkv = jax.nn.silu(causal_dwconv(jnp.concatenate([qa, ka, va], axis=-1), conv_qkv_w[l]))
        qa, ka, va = jnp.split(qkv, (KEY_A, 2 * KEY_A), axis=-1)
        qa = l2norm(qa.reshape(b_, t_, GDN_HEADS, GDN_DK))
        ka = l2norm(ka.reshape(b_, t_, GDN_HEADS, GDN_DK))
        va = va.reshape(b_, t_, GDN_HEADS, GDN_DV)
        beta = jax.nn.sigmoid(ba.astype(jnp.float32))
        g = -jnp.exp(a_log[l].astype(jnp.float32)) * jax.nn.softplus(
            aa.astype(jnp.float32) + dt_bias[l].astype(jnp.float32))
        oa = gated_delta_rule(qa, ka, va, g, beta)
        za = za.reshape(b_, t_, GDN_HEADS, GDN_DV).astype(jnp.float32)
        oa = (rmsnorm(oa, gdn_norm_w[l]) * jax.nn.silu(za)).astype(x.dtype).reshape(b_, t_, VAL_A)

        ob = chunk_band_attention(qb.reshape(b_, t_, ATT_HEADS, ATT_DH),
                                  kb.reshape(b_, t_, ATT_HEADS, ATT_DH),
                                  vb.reshape(b_, t_, ATT_HEADS, ATT_DH), rel_bias[l])

        mix = jax.nn.sigmoid(ga) * (oa @ w_branch_a[l]) + jax.nn.sigmoid(gb) * (ob @ w_branch_b[l])
        x = x + mix @ w_out[l]

        h = rmsnorm(x, norm_ffn_w[l])
        u = causal_dwconv(h @ w_up[l], conv_ffn_w[l]) + conv_ffn_b[l]
        gate, up = jnp.split(u, 2, axis=-1)
        x = x + (jax.nn.silu(gate) * up) @ w_down[l]
    return rmsnorm(x, norm_final_w)
```

```python
import functools

import jax
import jax.numpy as jnp
from jax import lax
from jax.experimental import pallas as pl
from jax.experimental.pallas import tpu as pltpu

CHUNK = 64
EPS = 1e-6
GDN_HEADS = 4
GDN_DK = 128
GDN_DV = 128
GDN_CONV = 4
ATT_HEADS = 8
ATT_DH = 64
ATT_BAND = 9
REL_CLIP = 128
FFN_CONV = 3

KEY_A = GDN_HEADS * GDN_DK
VAL_A = GDN_HEADS * GDN_DV
WIDTH_B = ATT_HEADS * ATT_DH
CONV_A = 2 * KEY_A + VAL_A
LEAD = (ATT_BAND - 1) * CHUNK
BAND = ATT_BAND * CHUNK

V7X_SUBLANES = 8
V7X_LANES = 128
V7X_VMEM_LIMIT_BYTES = 56 * 1024 * 1024

PROJ_ROWS = 256
GDN_CHUNKS = 4
ATT_ROWS = 512
FFN_ROWS = 256
FFN_COLS = 256

F32 = jnp.float32
BF16 = jnp.bfloat16


def _dot(a, b):
    return jnp.dot(a.astype(BF16), b.astype(BF16), preferred_element_type=F32)


def _dot_nt(a, b):
    return lax.dot_general(a.astype(BF16), b.astype(BF16), (((1,), (1,)), ((), ())),
                           preferred_element_type=F32)


def _dot_tn(a, b):
    return lax.dot_general(a.astype(BF16), b.astype(BF16), (((0,), (0,)), ((), ())),
                           preferred_element_type=F32)


def _sigmoid(x):
    return 1.0 / (1.0 + jnp.exp(-x))


def _softplus(x):
    return jnp.maximum(x, 0.0) + jnp.log1p(jnp.exp(-jnp.abs(x)))


def _rms_scale(x):
    return lax.rsqrt(jnp.mean(x * x, axis=-1, keepdims=True) + EPS)


def _const_spec(shape):
    zeros = (0,) * len(shape)
    return pl.BlockSpec(shape, lambda *_: zeros, pipeline_mode=pl.Buffered(1))


def _params(n_axes):
    return pltpu.CompilerParams(dimension_semantics=("arbitrary",) * n_axes,
                                vmem_limit_bytes=V7X_VMEM_LIMIT_BYTES)


def _proj_kernel(x_ref, nw_ref, wa_ref, wb_ref, wbg_ref, wbgt_ref, cw_ref, hpr_ref, hpc_ref,
                 a4_ref, qkvb_ref, gbc_ref, gbr_ref, cbuf_ref):
    tm = x_ref.shape[1]
    halo = V7X_SUBLANES
    x = x_ref[0]
    h = (x * _rms_scale(x) * nw_ref[...]).astype(BF16)

    @pl.when(pl.program_id(1) == 0)
    def _():
        cbuf_ref[0:halo, :] = jnp.zeros((halo, CONV_A), F32)

    pa = jnp.dot(h, wa_ref[...], preferred_element_type=F32)
    cbuf_ref[halo:, :] = pa[:, :CONV_A]
    a4_ref[0, :, CONV_A:] = pa[:, CONV_A:]
    for blk in range(CONV_A // V7X_LANES):
        cols = slice(blk * V7X_LANES, (blk + 1) * V7X_LANES)
        conv = cw_ref[GDN_CONV - 1:GDN_CONV, cols] * cbuf_ref[halo:halo + tm, cols]
        for tap in range(GDN_CONV - 1):
            off = halo - (GDN_CONV - 1) + tap
            conv = conv + cw_ref[tap:tap + 1, cols] * cbuf_ref[off:off + tm, cols]
        act = conv * _sigmoid(conv)
        if blk < 2 * GDN_HEADS:
            act = act * lax.rsqrt(jnp.sum(act * act, axis=-1, keepdims=True) + EPS)
        if blk < GDN_HEADS:
            act = act * (GDN_DK ** -0.5)
        a4_ref[0, :, cols] = act
    cbuf_ref[0:halo, :] = cbuf_ref[tm:tm + halo, :]

    pb = jnp.dot(h, wb_ref[...], preferred_element_type=F32)
    qkvb_ref[0, :, :WIDTH_B] = (pb[:, :WIDTH_B] * (ATT_DH ** -0.5)).astype(BF16)
    qkvb_ref[0, :, WIDTH_B:] = pb[:, WIDTH_B:].astype(BF16)

    pc = jnp.dot(h, wbg_ref[...], preferred_element_type=F32)
    lane = lax.broadcasted_iota(jnp.int32, pc.shape, 1)
    gc = -jnp.exp(hpr_ref[0:1, :]) * _softplus(pc + hpr_ref[1:2, :])
    valc = jnp.where(lane < GDN_HEADS, _sigmoid(pc), gc)
    gbc_ref[0] = valc.reshape(tm // CHUNK, CHUNK, 2 * GDN_HEADS)

    pr = lax.dot_general(wbgt_ref[...], h, (((1,), (1,)), ((), ())),
                         preferred_element_type=F32)
    row = lax.broadcasted_iota(jnp.int32, pr.shape, 0)
    gr = -jnp.exp(hpc_ref[:, 0:1]) * _softplus(pr + hpc_ref[:, 1:2])
    valr = jnp.where(row < GDN_HEADS, _sigmoid(pr), gr)
    for c in range(tm // CHUNK):
        gbr_ref[0, c] = valr[:, c * CHUNK:(c + 1) * CHUNK]


def _project(x, nw, wa, wb, wbg, wbgt, cw, hpr, hpc):
    b, t, d = x.shape
    tm = PROJ_ROWS
    nc = tm // CHUNK
    return pl.pallas_call(
        _proj_kernel,
        grid=(b, t // tm),
        in_specs=[
            pl.BlockSpec((1, tm, d), lambda bi, i: (bi, i, 0)),
            _const_spec(nw.shape), _const_spec(wa.shape), _const_spec(wb.shape),
            _const_spec(wbg.shape), _const_spec(wbgt.shape), _const_spec(cw.shape),
            _const_spec(hpr.shape), _const_spec(hpc.shape),
        ],
        out_specs=[
            pl.BlockSpec((1, tm, CONV_A + VAL_A), lambda bi, i: (bi, i, 0)),
            pl.BlockSpec((1, tm, 3 * WIDTH_B), lambda bi, i: (bi, i, 0)),
            pl.BlockSpec((1, nc, CHUNK, 2 * GDN_HEADS), lambda bi, i: (bi, i, 0, 0)),
            pl.BlockSpec((1, nc, 2 * GDN_HEADS, CHUNK), lambda bi, i: (bi, i, 0, 0)),
        ],
        out_shape=[
            jax.ShapeDtypeStruct((b, t, CONV_A + VAL_A), F32),
            jax.ShapeDtypeStruct((b, t, 3 * WIDTH_B), BF16),
            jax.ShapeDtypeStruct((b, t // CHUNK, CHUNK, 2 * GDN_HEADS), F32),
            jax.ShapeDtypeStruct((b, t // CHUNK, 2 * GDN_HEADS, CHUNK), F32),
        ],
        scratch_shapes=[pltpu.VMEM((tm + V7X_SUBLANES, CONV_A), F32)],
        compiler_params=_params(2),
        name="in_proj",
    )(x, nw, wa, wb, wbg, wbgt, cw, hpr, hpc)


def _unit_lower_inverse(a, ii, jj):
    same16 = (ii // 16) == (jj // 16)
    same32 = (ii // 32) == (jj // 32)
    eye = jnp.where(ii == jj, 1.0, 0.0).astype(F32)
    ad = jnp.where(same16, a, 0.0)
    inv = eye - ad
    power = _dot(ad, ad)
    inv = inv + _dot(power, inv)
    power = _dot(power, power)
    inv = inv + _dot(power, inv)
    power = _dot(power, power)
    inv = inv + _dot(power, inv)
    off32 = jnp.where(same32 & jnp.logical_not(same16), a, 0.0)
    inv = inv - _dot(inv, _dot(off32, inv))
    off64 = jnp.where(same32, 0.0, a)
    inv = inv - _dot(inv, _dot(off64, inv))
    return inv


def _gdn_kernel(a4_ref, gbc_ref, gbr_ref, gnw_ref, oa_ref, s_ref):
    nb = a4_ref.shape[0]
    nc = gbc_ref.shape[1]

    @pl.when(pl.program_id(0) == 0)
    def _():
        s_ref[...] = jnp.zeros(s_ref.shape, F32)

    ii = lax.broadcasted_iota(jnp.int32, (CHUNK, CHUNK), 0)
    jj = lax.broadcasted_iota(jnp.int32, (CHUNK, CHUNK), 1)
    incl = ii >= jj
    strict = ii > jj
    tri_lo = jnp.where(incl, 1.0, 0.0).astype(F32)
    tri_up = jnp.where(ii <= jj, 1.0, 0.0).astype(F32)
    gnw = gnw_ref[...]

    def chunk_step(c, carry):
        r0 = pl.multiple_of(c * CHUNK, CHUNK)
        rows = pl.ds(r0, CHUNK)
        for bi in range(nb):
            gcol = jnp.dot(tri_lo, gbc_ref[bi, c], precision=lax.Precision.HIGHEST,
                           preferred_element_type=F32)
            grow = jnp.dot(gbr_ref[bi, c], tri_up, precision=lax.Precision.HIGHEST,
                           preferred_element_type=F32)
            bcol = gbc_ref[bi, c]
            for hd in range(GDN_HEADS):
                lanes = slice(hd * GDN_DK, (hd + 1) * GDN_DK)
                q = a4_ref[bi, rows, lanes]
                k = a4_ref[bi, rows, KEY_A + hd * GDN_DK:KEY_A + (hd + 1) * GDN_DK]
                v = a4_ref[bi, rows, 2 * KEY_A + hd * GDN_DV:2 * KEY_A + (hd + 1) * GDN_DV]
                z = a4_ref[bi, rows, CONV_A + hd * GDN_DV:CONV_A + (hd + 1) * GDN_DV]
                g_c = gcol[:, GDN_HEADS + hd:GDN_HEADS + hd + 1]
                g_r = grow[GDN_HEADS + hd:GDN_HEADS + hd + 1, :]
                beta = bcol[:, hd:hd + 1]
                g_last = g_c[CHUNK - 1:CHUNK, :]
                dec = jnp.exp(jnp.where(incl, g_c - g_r, 0.0))
                gam = jnp.exp(g_c)

                kq = _dot_nt(jnp.concatenate([k, q], axis=0), k)
                a_mat = jnp.where(strict, beta * kq[:CHUNK] * dec, 0.0)
                p_mat = jnp.where(incl, kq[CHUNK:] * dec, 0.0)
                inv = _unit_lower_inverse(a_mat, ii, jj)
                rhs = jnp.concatenate([(beta * gam) * k, beta * v], axis=1)
                sol = _dot(inv, rhs)
                w = sol[:, :GDN_DK]
                uv = sol[:, GDN_DK:]

                idx = bi * GDN_HEADS + hd
                s = s_ref[idx]
                ws = _dot(jnp.concatenate([w, q * gam], axis=0), s)
                u = uv - ws[:CHUNK]
                o = ws[CHUNK:] + _dot(p_mat, u)
                kd = k * jnp.exp(g_last - g_c)
                s_ref[idx] = jnp.exp(g_last) * s + _dot_tn(kd, u)

                o = o * _rms_scale(o) * gnw
                oa_ref[bi, rows, hd * GDN_DV:(hd + 1) * GDN_DV] = (o * (z * _sigmoid(z))).astype(BF16)
        return carry

    lax.fori_loop(0, nc, chunk_step, 0)


def _gated_delta(a4, gbc, gbr, gnw):
    b, t, _ = a4.shape
    nc = GDN_CHUNKS
    rows = nc * CHUNK
    return pl.pallas_call(
        _gdn_kernel,
        grid=(t // rows,),
        in_specs=[
            pl.BlockSpec((b, rows, CONV_A + VAL_A), lambda i: (0, i, 0)),
            pl.BlockSpec((b, nc, CHUNK, 2 * GDN_HEADS), lambda i: (0, i, 0, 0)),
            pl.BlockSpec((b, nc, 2 * GDN_HEADS, CHUNK), lambda i: (0, i, 0, 0)),
            _const_spec(gnw.shape),
        ],
        out_specs=pl.BlockSpec((b, rows, VAL_A), lambda i: (0, i, 0)),
        out_shape=jax.ShapeDtypeStruct((b, t, VAL_A), BF16),
        scratch_shapes=[pltpu.VMEM((b * GDN_HEADS, GDN_DK, GDN_DV), F32)],
        compiler_params=_params(1),
        name="gated_delta",
    )(a4, gbc, gbr, gnw)


def _attn_kernel(q_ref, k_ref, v_ref, bias_ref, ob_ref, kbuf_ref, vbuf_ref):
    tq = q_ref.shape[1]
    i = pl.program_id(1)

    @pl.when(i == 0)
    def _():
        kbuf_ref[0:LEAD, :] = jnp.zeros((LEAD, WIDTH_B), BF16)
        vbuf_ref[0:LEAD, :] = jnp.zeros((LEAD, WIDTH_B), BF16)

    @pl.when(i > 0)
    def _():
        kbuf_ref[0:LEAD, :] = kbuf_ref[tq:tq + LEAD, :]
        vbuf_ref[0:LEAD, :] = vbuf_ref[tq:tq + LEAD, :]

    kbuf_ref[LEAD:, :] = k_ref[0]
    vbuf_ref[LEAD:, :] = v_ref[0]

    lane = lax.broadcasted_iota(jnp.int32, (CHUNK, 2 * ATT_DH), 1)
    first = lane < ATT_DH
    key = lax.broadcasted_iota(jnp.int32, (1, BAND), 1)
    zero_q = jnp.zeros((CHUNK, 2 * ATT_DH), BF16)

    def chunk_step(c, carry):
        r0 = pl.multiple_of(c * CHUNK, CHUNK)
        valid = (i * tq - LEAD + c * CHUNK + key) >= 0
        for pair in range(ATT_HEADS // 2):
            lanes = slice(pair * 2 * ATT_DH, (pair + 1) * 2 * ATT_DH)
            q = q_ref[0, pl.ds(r0, CHUNK), lanes]
            kb = kbuf_ref[pl.ds(r0, BAND), lanes]
            vb = vbuf_ref[pl.ds(r0, BAND), lanes]
            q2 = jnp.concatenate([jnp.where(first, q, zero_q), jnp.where(first, zero_q, q)], axis=0)
            s = _dot_nt(q2, kb) + bias_ref[pair]
            s = jnp.where(valid, s, -jnp.inf)
            p = jnp.exp(s - jnp.max(s, axis=-1, keepdims=True))
            denom = jnp.sum(p, axis=-1, keepdims=True)
            pv = _dot(p, vb) / denom
            ob_ref[0, pl.ds(r0, CHUNK), lanes] = jnp.where(first, pv[:CHUNK], pv[CHUNK:]).astype(BF16)
        return carry

    lax.fori_loop(0, tq // CHUNK, chunk_step, 0)


def _band_attention(qkvb, bias2):
    b, t, _ = qkvb.shape
    tq = ATT_ROWS
    return pl.pallas_call(
        _attn_kernel,
        grid=(b, t // tq),
        in_specs=[
            pl.BlockSpec((1, tq, WIDTH_B), lambda bi, i: (bi, i, 0)),
            pl.BlockSpec((1, tq, WIDTH_B), lambda bi, i: (bi, i, 1)),
            pl.BlockSpec((1, tq, WIDTH_B), lambda bi, i: (bi, i, 2)),
            _const_spec(bias2.shape),
        ],
        out_specs=pl.BlockSpec((1, tq, WIDTH_B), lambda bi, i: (bi, i, 0)),
        out_shape=jax.ShapeDtypeStruct((b, t, WIDTH_B), BF16),
        scratch_shapes=[pltpu.VMEM((LEAD + tq, WIDTH_B), BF16),
                        pltpu.VMEM((LEAD + tq, WIDTH_B), BF16)],
        compiler_params=_params(2),
        name="band_attention",
    )(qkvb, qkvb, qkvb, bias2)


def _ffn_kernel(final_norm, x_ref, oa_ref, ob_ref, nmw_ref, wg_ref, wa_ref, wb_ref, wo_ref,
                nfw_ref, wup_ref, cfw_ref, cfb_ref, wdn_ref, nlw_ref, out_ref, ubuf_ref, act_ref):
    tm = x_ref.shape[1]
    d = x_ref.shape[2]
    dff = wdn_ref.shape[0]
    halo = V7X_SUBLANES
    x = x_ref[0]

    h = (x * _rms_scale(x) * nmw_ref[...]).astype(BF16)
    gates = jnp.dot(h, wg_ref[...], preferred_element_type=F32)
    ya = jnp.dot(oa_ref[0], wa_ref[...], preferred_element_type=F32)
    yb = jnp.dot(ob_ref[0], wb_ref[...], preferred_element_type=F32)
    mix = _sigmoid(gates[:, :d]) * ya + _sigmoid(gates[:, d:]) * yb
    x1 = x + _dot(mix, wo_ref[...])

    @pl.when(pl.program_id(1) == 0)
    def _():
        ubuf_ref[0:halo, :] = jnp.zeros((halo, 2 * dff), F32)

    h2 = (x1 * _rms_scale(x1) * nfw_ref[...]).astype(BF16)

    def conv_block(cols):
        ubuf_ref[halo:, cols] = jnp.dot(h2, wup_ref[:, cols], preferred_element_type=F32)
        acc = cfb_ref[:, cols] + cfw_ref[FFN_CONV - 1:FFN_CONV, cols] * ubuf_ref[halo:halo + tm, cols]
        for tap in range(FFN_CONV - 1):
            off = halo - (FFN_CONV - 1) + tap
            acc = acc + cfw_ref[tap:tap + 1, cols] * ubuf_ref[off:off + tm, cols]
        ubuf_ref[0:halo, cols] = ubuf_ref[tm:tm + halo, cols]
        return acc

    for blk in range(dff // FFN_COLS):
        gate = conv_block(slice(blk * FFN_COLS, (blk + 1) * FFN_COLS))
        up = conv_block(slice(dff + blk * FFN_COLS, dff + (blk + 1) * FFN_COLS))
        act_ref[:, blk * FFN_COLS:(blk + 1) * FFN_COLS] = (gate * _sigmoid(gate) * up).astype(BF16)

    x2 = x1 + jnp.dot(act_ref[...], wdn_ref[...], preferred_element_type=F32)
    if final_norm:
        x2 = x2 * _rms_scale(x2) * nlw_ref[...]
    out_ref[0] = x2


def _merge_ffn(x, oa, ob, nmw, wg, wa, wb, wo, nfw, wup, cfw, cfb, wdn, nlw, final_norm):
    b, t, d = x.shape
    tm = FFN_ROWS
    dff = wdn.shape[0]
    consts = (nmw, wg, wa, wb, wo, nfw, wup, cfw, cfb, wdn, nlw)
    return pl.pallas_call(
        functools.partial(_ffn_kernel, final_norm),
        grid=(b, t // tm),
        in_specs=[
            pl.BlockSpec((1, tm, d), lambda bi, i: (bi, i, 0)),
            pl.BlockSpec((1, tm, VAL_A), lambda bi, i: (bi, i, 0)),
            pl.BlockSpec((1, tm, WIDTH_B), lambda bi, i: (bi, i, 0)),
        ] + [_const_spec(c.shape) for c in consts],
        out_specs=pl.BlockSpec((1, tm, d), lambda bi, i: (bi, i, 0)),
        out_shape=jax.ShapeDtypeStruct((b, t, d), F32),
        scratch_shapes=[pltpu.VMEM((tm + V7X_SUBLANES, 2 * dff), F32),
                        pltpu.VMEM((tm, dff), BF16)],
        compiler_params=_params(2),
        name="merge_ffn",
    )(x, oa, ob, *consts)


def _rel_bias_table(rel_table):
    r = jnp.arange(CHUNK)
    j = jnp.arange(BAND)
    dist = LEAD + r[:, None] - j[None, :]
    bias = rel_table.astype(F32)[:, jnp.clip(dist, -REL_CLIP, REL_CLIP) + REL_CLIP]
    return bias.reshape(ATT_HEADS // 2, 2 * CHUNK, BAND)


def kernel(x, norm_mix_w, w_in, conv_qkv_w, a_log, dt_bias, gdn_norm_w, w_branch_a, w_branch_b,
           rel_bias, w_out, norm_ffn_w, w_up, conv_ffn_w, conv_ffn_b, w_down, norm_final_w):
    depth = w_in.shape[0]
    d = x.shape[-1]
    o_z = CONV_A + VAL_A
    o_bg = o_z + 2 * GDN_HEADS
    o_b = o_bg + 3 * WIDTH_B
    zeros_h = jnp.zeros((GDN_HEADS,), F32)
    for l in range(depth):
        w = w_in[l].astype(BF16)
        wbg = w[:, o_z:o_bg]
        hpr = jnp.stack([jnp.concatenate([zeros_h, a_log[l].astype(F32)]),
                         jnp.concatenate([zeros_h, dt_bias[l].astype(F32)])])
        a4, qkvb, gbc, gbr = _project(
            x, norm_mix_w[l][None, :], w[:, :o_z], w[:, o_bg:o_b], wbg, wbg.T,
            conv_qkv_w[l], hpr, hpr.T)
        oa = _gated_delta(a4, gbc, gbr, gdn_norm_w[l][None, :])
        ob = _band_attention(qkvb, _rel_bias_table(rel_bias[l]))
        x = _merge_ffn(
            x, oa, ob, norm_mix_w[l][None, :], w[:, o_b:], w_branch_a[l].astype(BF16),
            w_branch_b[l].astype(BF16), w_out[l].astype(BF16), norm_ffn_w[l][None, :],
            w_up[l].astype(BF16), conv_ffn_w[l], conv_ffn_b[l][None, :], w_down[l].astype(BF16),
            norm_final_w[None, :], l == depth - 1)
    return x
```

```python
import functools

import jax
import jax.numpy as jnp
from jax import lax
from jax.experimental import pallas as pl
from jax.experimental.pallas import tpu as pltpu

CHUNK = 64
EPS = 1e-6
GDN_HEADS = 4
GDN_DK = 128
GDN_DV = 128
GDN_CONV = 4
ATT_HEADS = 8
ATT_DH = 64
ATT_BAND = 9
REL_CLIP = 128
FFN_CONV = 3

KEY_A = GDN_HEADS * GDN_DK
VAL_A = GDN_HEADS * GDN_DV
WIDTH_B = ATT_HEADS * ATT_DH
CONV_A = 2 * KEY_A + VAL_A
LEAD = (ATT_BAND - 1) * CHUNK
BAND = ATT_BAND * CHUNK

V7X_SUBLANES = 8
V7X_LANES = 128
V7X_VMEM_LIMIT_BYTES = 56 * 1024 * 1024

BIAS_ROLL = -(-(BAND + CHUNK - 1) // V7X_LANES) * V7X_LANES

PROJ_ROWS = 256
GDN_CHUNKS = 4
ATT_ROWS = 512
FFN_ROWS = 256
FFN_COLS = 256

F32 = jnp.float32
BF16 = jnp.bfloat16


def _dot(a, b):
    return jnp.dot(a.astype(BF16), b.astype(BF16), preferred_element_type=F32)


def _dot_nt(a, b):
    return lax.dot_general(a.astype(BF16), b.astype(BF16), (((1,), (1,)), ((), ())),
                           preferred_element_type=F32)


def _dot_tn(a, b):
    return lax.dot_general(a.astype(BF16), b.astype(BF16), (((0,), (0,)), ((), ())),
                           preferred_element_type=F32)


def _sigmoid(x):
    return 1.0 / (1.0 + jnp.exp(-x))


def _softplus(x):
    return jnp.maximum(x, 0.0) + jnp.log1p(jnp.exp(-jnp.abs(x)))


def _rms_scale(x):
    return lax.rsqrt(jnp.mean(x * x, axis=-1, keepdims=True) + EPS)


def _const_spec(shape):
    zeros = (0,) * len(shape)
    return pl.BlockSpec(shape, lambda *_: zeros, pipeline_mode=pl.Buffered(1))


def _params(n_axes):
    return pltpu.CompilerParams(dimension_semantics=("arbitrary",) * n_axes,
                                vmem_limit_bytes=V7X_VMEM_LIMIT_BYTES)


def _proj_kernel(x_ref, nw_ref, wa_ref, wb_ref, wbg_ref, wbgt_ref, cw_ref, hpr_ref, hpc_ref,
                 a4_ref, qkvb_ref, gbc_ref, gbr_ref, cbuf_ref):
    tm = x_ref.shape[1]
    halo = V7X_SUBLANES
    x = x_ref[0]
    h = (x * _rms_scale(x) * nw_ref[...]).astype(BF16)

    @pl.when(pl.program_id(1) == 0)
    def _():
        cbuf_ref[0:halo, :] = jnp.zeros((halo, CONV_A), F32)

    pa = jnp.dot(h, wa_ref[...], preferred_element_type=F32)
    cbuf_ref[halo:, :] = pa[:, :CONV_A]
    a4_ref[0, :, CONV_A:] = pa[:, CONV_A:]
    for blk in range(CONV_A // V7X_LANES):
        cols = slice(blk * V7X_LANES, (blk + 1) * V7X_LANES)
        conv = cw_ref[GDN_CONV - 1:GDN_CONV, cols] * cbuf_ref[halo:halo + tm, cols]
        for tap in range(GDN_CONV - 1):
            off = halo - (GDN_CONV - 1) + tap
            conv = conv + cw_ref[tap:tap + 1, cols] * cbuf_ref[off:off + tm, cols]
        act = conv * _sigmoid(conv)
        if blk < 2 * GDN_HEADS:
            act = act * lax.rsqrt(jnp.sum(act * act, axis=-1, keepdims=True) + EPS)
        if blk < GDN_HEADS:
            act = act * (GDN_DK ** -0.5)
        a4_ref[0, :, cols] = act
    cbuf_ref[0:halo, :] = cbuf_ref[tm:tm + halo, :]

    pb = jnp.dot(h, wb_ref[...], preferred_element_type=F32)
    qkvb_ref[0, :, :WIDTH_B] = (pb[:, :WIDTH_B] * (ATT_DH ** -0.5)).astype(BF16)
    qkvb_ref[0, :, WIDTH_B:] = pb[:, WIDTH_B:].astype(BF16)

    pc = jnp.dot(h, wbg_ref[...], preferred_element_type=F32)
    lane = lax.broadcasted_iota(jnp.int32, pc.shape, 1)
    gc = -jnp.exp(hpr_ref[0:1, :]) * _softplus(pc + hpr_ref[1:2, :])
    valc = jnp.where(lane < GDN_HEADS, _sigmoid(pc), gc)
    gbc_ref[0] = valc.reshape(tm // CHUNK, CHUNK, 2 * GDN_HEADS)

    pr = lax.dot_general(wbgt_ref[...], h, (((1,), (1,)), ((), ())),
                         preferred_element_type=F32)
    row = lax.broadcasted_iota(jnp.int32, pr.shape, 0)
    gr = -jnp.exp(hpc_ref[:, 0:1]) * _softplus(pr + hpc_ref[:, 1:2])
    valr = jnp.where(row < GDN_HEADS, _sigmoid(pr), gr)
    for c in range(tm // CHUNK):
        gbr_ref[0, c] = valr[:, c * CHUNK:(c + 1) * CHUNK]


def _project(x, nw, wa, wb, wbg, wbgt, cw, hpr, hpc):
    b, t, d = x.shape
    tm = PROJ_ROWS
    nc = tm // CHUNK
    return pl.pallas_call(
        _proj_kernel,
        grid=(b, t // tm),
        in_specs=[
            pl.BlockSpec((1, tm, d), lambda bi, i: (bi, i, 0)),
            _const_spec(nw.shape), _const_spec(wa.shape), _const_spec(wb.shape),
            _const_spec(wbg.shape), _const_spec(wbgt.shape), _const_spec(cw.shape),
            _const_spec(hpr.shape), _const_spec(hpc.shape),
        ],
        out_specs=[
            pl.BlockSpec((1, tm, CONV_A + VAL_A), lambda bi, i: (bi, i, 0)),
            pl.BlockSpec((1, tm, 3 * WIDTH_B), lambda bi, i: (bi, i, 0)),
            pl.BlockSpec((1, nc, CHUNK, 2 * GDN_HEADS), lambda bi, i: (bi, i, 0, 0)),
            pl.BlockSpec((1, nc, 2 * GDN_HEADS, CHUNK), lambda bi, i: (bi, i, 0, 0)),
        ],
        out_shape=[
            jax.ShapeDtypeStruct((b, t, CONV_A + VAL_A), F32),
            jax.ShapeDtypeStruct((b, t, 3 * WIDTH_B), BF16),
            jax.ShapeDtypeStruct((b, t // CHUNK, CHUNK, 2 * GDN_HEADS), F32),
            jax.ShapeDtypeStruct((b, t // CHUNK, 2 * GDN_HEADS, CHUNK), F32),
        ],
        scratch_shapes=[pltpu.VMEM((tm + V7X_SUBLANES, CONV_A), F32)],
        compiler_params=_params(2),
        name="in_proj",
    )(x, nw, wa, wb, wbg, wbgt, cw, hpr, hpc)


def _each(fn, *lists):
    return [fn(*args) for args in zip(*lists)]


def _unit_lower_inverse(a, ii, jj):
    same16 = (ii // 16) == (jj // 16)
    same32 = (ii // 32) == (jj // 32)
    eye = jnp.where(ii == jj, 1.0, 0.0).astype(F32)
    ad = _each(lambda x: jnp.where(same16, x, 0.0), a)
    inv = _each(lambda x: eye - x, ad)
    power = _each(_dot, ad, ad)
    for level in range(3):
        inv = _each(lambda i, pi: i + pi, inv, _each(_dot, power, inv))
        if level < 2:
            power = _each(_dot, power, power)
    off32 = _each(lambda x: jnp.where(same32 & jnp.logical_not(same16), x, 0.0), a)
    inv = _each(lambda i, t: i - t, inv, _each(_dot, inv, _each(_dot, off32, inv)))
    off64 = _each(lambda x: jnp.where(same32, 0.0, x), a)
    inv = _each(lambda i, t: i - t, inv, _each(_dot, inv, _each(_dot, off64, inv)))
    return inv


def _gdn_kernel(a4_ref, gbc_ref, gbr_ref, gnw_ref, oa_ref, s_ref):
    nb = a4_ref.shape[0]
    nc = gbc_ref.shape[1]

    @pl.when(pl.program_id(0) == 0)
    def _():
        s_ref[...] = jnp.zeros(s_ref.shape, F32)

    ii = lax.broadcasted_iota(jnp.int32, (CHUNK, CHUNK), 0)
    jj = lax.broadcasted_iota(jnp.int32, (CHUNK, CHUNK), 1)
    incl = ii >= jj
    strict = ii > jj
    tri_lo = jnp.where(incl, 1.0, 0.0).astype(F32)
    tri_up = jnp.where(ii <= jj, 1.0, 0.0).astype(F32)
    gnw = gnw_ref[...]

    def chunk_step(c, carry):
        r0 = pl.multiple_of(c * CHUNK, CHUNK)
        rows = pl.ds(r0, CHUNK)
        chains = [(bi, hd) for bi in range(nb) for hd in range(GDN_HEADS)]

        gcol = [jnp.dot(tri_lo, gbc_ref[bi, c], precision=lax.Precision.HIGHEST,
                        preferred_element_type=F32) for bi in range(nb)]
        grow = [jnp.dot(gbr_ref[bi, c], tri_up, precision=lax.Precision.HIGHEST,
                        preferred_element_type=F32) for bi in range(nb)]

        def load(col0):
            return [a4_ref[bi, rows, col0 + hd * GDN_DK:col0 + (hd + 1) * GDN_DK] for bi, hd in chains]

        q, k, v = load(0), load(KEY_A), load(2 * KEY_A)
        g_c = [gcol[bi][:, GDN_HEADS + hd:GDN_HEADS + hd + 1] for bi, hd in chains]
        g_r = [grow[bi][GDN_HEADS + hd:GDN_HEADS + hd + 1, :] for bi, hd in chains]
        beta = [gbc_ref[bi, c][:, hd:hd + 1] for bi, hd in chains]
        g_last = [x[CHUNK - 1:CHUNK, :] for x in g_c]
        dec = _each(lambda gc, gr: jnp.exp(jnp.where(incl, gc - gr, 0.0)), g_c, g_r)
        gam = _each(jnp.exp, g_c)

        kq = _each(lambda k_, q_: _dot_nt(jnp.concatenate([k_, q_], axis=0), k_), k, q)
        a_mat = _each(lambda b_, x, d_: jnp.where(strict, b_ * x[:CHUNK] * d_, 0.0), beta, kq, dec)
        p_mat = _each(lambda x, d_: jnp.where(incl, x[CHUNK:] * d_, 0.0), kq, dec)
        inv = _unit_lower_inverse(a_mat, ii, jj)
        rhs = _each(lambda b_, g_, k_, v_: jnp.concatenate([(b_ * g_) * k_, b_ * v_], axis=1),
                    beta, gam, k, v)
        sol = _each(_dot, inv, rhs)

        s = [s_ref[bi * GDN_HEADS + hd] for bi, hd in chains]
        ws = _each(lambda sol_, q_, g_, s_: _dot(jnp.concatenate([sol_[:, :GDN_DK], q_ * g_], axis=0), s_),
                   sol, q, gam, s)
        u = _each(lambda sol_, ws_: sol_[:, GDN_DK:] - ws_[:CHUNK], sol, ws)
        kd = _each(lambda k_, gl, gc: k_ * jnp.exp(gl - gc), k, g_last, g_c)
        pu = _each(_dot, p_mat, u)
        ku = _each(_dot_tn, kd, u)
        for (bi, hd), gl, s_, ku_ in zip(chains, g_last, s, ku):
            s_ref[bi * GDN_HEADS + hd] = jnp.exp(gl) * s_ + ku_

        z = load(CONV_A)
        for (bi, hd), ws_, pu_, z_ in zip(chains, ws, pu, z):
            o = ws_[CHUNK:] + pu_
            o = o * _rms_scale(o) * gnw
            oa_ref[bi, rows, hd * GDN_DV:(hd + 1) * GDN_DV] = (o * (z_ * _sigmoid(z_))).astype(BF16)
        return carry

    lax.fori_loop(0, nc, chunk_step, 0)


def _gated_delta(a4, gbc, gbr, gnw):
    b, t, _ = a4.shape
    nc = GDN_CHUNKS
    rows = nc * CHUNK
    return pl.pallas_call(
        _gdn_kernel,
        grid=(t // rows,),
        in_specs=[
            pl.BlockSpec((b, rows, CONV_A + VAL_A), lambda i: (0, i, 0)),
            pl.BlockSpec((b, nc, CHUNK, 2 * GDN_HEADS), lambda i: (0, i, 0, 0)),
            pl.BlockSpec((b, nc, 2 * GDN_HEADS, CHUNK), lambda i: (0, i, 0, 0)),
            _const_spec(gnw.shape),
        ],
        out_specs=pl.BlockSpec((b, rows, VAL_A), lambda i: (0, i, 0)),
        out_shape=jax.ShapeDtypeStruct((b, t, VAL_A), BF16),
        scratch_shapes=[pltpu.VMEM((b * GDN_HEADS, GDN_DK, GDN_DV), F32)],
        compiler_params=_params(1),
        name="gated_delta",
    )(a4, gbc, gbr, gnw)


def _attn_kernel(q_ref, k_ref, v_ref, rel_ref, ob_ref, kbuf_ref, vbuf_ref, bias_ref):
    tq = q_ref.shape[1]
    i = pl.program_id(1)

    @pl.when((i == 0) & (pl.program_id(0) == 0))
    def _():
        for hd in range(ATT_HEADS):
            rel = jnp.broadcast_to(rel_ref[hd:hd + 1, :], (CHUNK, BIAS_ROLL))
            rolled = pltpu.roll(rel, 0, 1, stride=1, stride_axis=0)
            bias_ref[hd // 2, (hd % 2) * CHUNK:(hd % 2 + 1) * CHUNK, :] = rolled[:, :BAND]

    @pl.when(i == 0)
    def _():
        kbuf_ref[0:LEAD, :] = jnp.zeros((LEAD, WIDTH_B), BF16)
        vbuf_ref[0:LEAD, :] = jnp.zeros((LEAD, WIDTH_B), BF16)

    @pl.when(i > 0)
    def _():
        kbuf_ref[0:LEAD, :] = kbuf_ref[tq:tq + LEAD, :]
        vbuf_ref[0:LEAD, :] = vbuf_ref[tq:tq + LEAD, :]

    kbuf_ref[LEAD:, :] = k_ref[0]
    vbuf_ref[LEAD:, :] = v_ref[0]

    lane = lax.broadcasted_iota(jnp.int32, (CHUNK, 2 * ATT_DH), 1)
    first = lane < ATT_DH
    key = lax.broadcasted_iota(jnp.int32, (1, BAND), 1)
    zero_q = jnp.zeros((CHUNK, 2 * ATT_DH), BF16)

    def chunk_step(c, carry):
        r0 = pl.multiple_of(c * CHUNK, CHUNK)
        valid = (i * tq - LEAD + c * CHUNK + key) >= 0
        pairs = list(range(ATT_HEADS // 2))
        lanes = [slice(pr * 2 * ATT_DH, (pr + 1) * 2 * ATT_DH) for pr in pairs]

        def stacked_q(ln):
            q = q_ref[0, pl.ds(r0, CHUNK), ln]
            return jnp.concatenate([jnp.where(first, q, zero_q), jnp.where(first, zero_q, q)], axis=0)

        s = [_dot_nt(stacked_q(ln), kbuf_ref[pl.ds(r0, BAND), ln]) for ln in lanes]
        s = [jnp.where(valid, x + bias_ref[pr], -jnp.inf) for x, pr in zip(s, pairs)]
        p = [jnp.exp(x - jnp.max(x, axis=-1, keepdims=True)) for x in s]
        denom = [jnp.sum(x, axis=-1, keepdims=True) for x in p]
        pv = [_dot(x, vbuf_ref[pl.ds(r0, BAND), ln]) for x, ln in zip(p, lanes)]
        for x, dn, ln in zip(pv, denom, lanes):
            x = x / dn
            ob_ref[0, pl.ds(r0, CHUNK), ln] = jnp.where(first, x[:CHUNK], x[CHUNK:]).astype(BF16)
        return carry

    lax.fori_loop(0, tq // CHUNK, chunk_step, 0)


def _band_attention(qkvb, rel):
    b, t, _ = qkvb.shape
    tq = ATT_ROWS
    return pl.pallas_call(
        _attn_kernel,
        grid=(b, t // tq),
        in_specs=[
            pl.BlockSpec((1, tq, WIDTH_B), lambda bi, i: (bi, i, 0)),
            pl.BlockSpec((1, tq, WIDTH_B), lambda bi, i: (bi, i, 1)),
            pl.BlockSpec((1, tq, WIDTH_B), lambda bi, i: (bi, i, 2)),
            _const_spec(rel.shape),
        ],
        out_specs=pl.BlockSpec((1, tq, WIDTH_B), lambda bi, i: (bi, i, 0)),
        out_shape=jax.ShapeDtypeStruct((b, t, WIDTH_B), BF16),
        scratch_shapes=[pltpu.VMEM((LEAD + tq, WIDTH_B), BF16),
                        pltpu.VMEM((LEAD + tq, WIDTH_B), BF16),
                        pltpu.VMEM((ATT_HEADS // 2, 2 * CHUNK, BAND), F32)],
        compiler_params=_params(2),
        name="band_attention",
    )(qkvb, qkvb, qkvb, rel)


def _ffn_kernel(final_norm, x_ref, oa_ref, ob_ref, nmw_ref, wg_ref, wa_ref, wb_ref, wo_ref,
                nfw_ref, wup_ref, cfw_ref, cfb_ref, wdn_ref, nlw_ref, out_ref, ubuf_ref, act_ref):
    tm = x_ref.shape[1]
    d = x_ref.shape[2]
    dff = wdn_ref.shape[0]
    halo = V7X_SUBLANES
    x = x_ref[0]

    h = (x * _rms_scale(x) * nmw_ref[...]).astype(BF16)
    gates = jnp.dot(h, wg_ref[...], preferred_element_type=F32)
    ya = jnp.dot(oa_ref[0], wa_ref[...], preferred_element_type=F32)
    yb = jnp.dot(ob_ref[0], wb_ref[...], preferred_element_type=F32)
    mix = _sigmoid(gates[:, :d]) * ya + _sigmoid(gates[:, d:]) * yb
    x1 = x + _dot(mix, wo_ref[...])

    @pl.when(pl.program_id(1) == 0)
    def _():
        ubuf_ref[0:halo, :] = jnp.zeros((halo, 2 * dff), F32)

    h2 = (x1 * _rms_scale(x1) * nfw_ref[...]).astype(BF16)

    def conv_block(cols):
        ubuf_ref[halo:, cols] = jnp.dot(h2, wup_ref[:, cols], preferred_element_type=F32)
        acc = cfb_ref[:, cols] + cfw_ref[FFN_CONV - 1:FFN_CONV, cols] * ubuf_ref[halo:halo + tm, cols]
        for tap in range(FFN_CONV - 1):
            off = halo - (FFN_CONV - 1) + tap
            acc = acc + cfw_ref[tap:tap + 1, cols] * ubuf_ref[off:off + tm, cols]
        ubuf_ref[0:halo, cols] = ubuf_ref[tm:tm + halo, cols]
        return acc

    for blk in range(dff // FFN_COLS):
        gate = conv_block(slice(blk * FFN_COLS, (blk + 1) * FFN_COLS))
        up = conv_block(slice(dff + blk * FFN_COLS, dff + (blk + 1) * FFN_COLS))
        act_ref[:, blk * FFN_COLS:(blk + 1) * FFN_COLS] = (gate * _sigmoid(gate) * up).astype(BF16)

    x2 = x1 + jnp.dot(act_ref[...], wdn_ref[...], preferred_element_type=F32)
    if final_norm:
        x2 = x2 * _rms_scale(x2) * nlw_ref[...]
    out_ref[0] = x2


def _merge_ffn(x, oa, ob, nmw, wg, wa, wb, wo, nfw, wup, cfw, cfb, wdn, nlw, final_norm):
    b, t, d = x.shape
    tm = FFN_ROWS
    dff = wdn.shape[0]
    consts = (nmw, wg, wa, wb, wo, nfw, wup, cfw, cfb, wdn, nlw)
    return pl.pallas_call(
        functools.partial(_ffn_kernel, final_norm),
        grid=(b, t // tm),
        in_specs=[
            pl.BlockSpec((1, tm, d), lambda bi, i: (bi, i, 0)),
            pl.BlockSpec((1, tm, VAL_A), lambda bi, i: (bi, i, 0)),
            pl.BlockSpec((1, tm, WIDTH_B), lambda bi, i: (bi, i, 0)),
        ] + [_const_spec(c.shape) for c in consts],
        out_specs=pl.BlockSpec((1, tm, d), lambda bi, i: (bi, i, 0)),
        out_shape=jax.ShapeDtypeStruct((b, t, d), F32),
        scratch_shapes=[pltpu.VMEM((tm + V7X_SUBLANES, 2 * dff), F32),
                        pltpu.VMEM((tm, dff), BF16)],
        compiler_params=_params(2),
        name="merge_ffn",
    )(x, oa, ob, *consts)


def _rel_bias_row(rel_table):
    n = jnp.arange(BIAS_ROLL)
    dist = jnp.where(n < BAND, jnp.clip(LEAD - n, -REL_CLIP, REL_CLIP), REL_CLIP)
    return rel_table.astype(F32)[:, dist + REL_CLIP]


def kernel(x, norm_mix_w, w_in, conv_qkv_w, a_log, dt_bias, gdn_norm_w, w_branch_a, w_branch_b,
           rel_bias, w_out, norm_ffn_w, w_up, conv_ffn_w, conv_ffn_b, w_down, norm_final_w):
    depth = w_in.shape[0]
    d = x.shape[-1]
    o_z = CONV_A + VAL_A
    o_bg = o_z + 2 * GDN_HEADS
    o_b = o_bg + 3 * WIDTH_B
    zeros_h = jnp.zeros((GDN_HEADS,), F32)
    for l in range(depth):
        w = w_in[l].astype(BF16)
        wbg = w[:, o_z:o_bg]
        hpr = jnp.stack([jnp.concatenate([zeros_h, a_log[l].astype(F32)]),
                         jnp.concatenate([zeros_h, dt_bias[l].astype(F32)])])
        a4, qkvb, gbc, gbr = _project(
            x, norm_mix_w[l][None, :], w[:, :o_z], w[:, o_bg:o_b], wbg, wbg.T,
            conv_qkv_w[l], hpr, hpr.T)
        oa = _gated_delta(a4, gbc, gbr, gdn_norm_w[l][None, :])
        ob = _band_attention(qkvb, _rel_bias_row(rel_bias[l]))
        x = _merge_ffn(
            x, oa, ob, norm_mix_w[l][None, :], w[:, o_b:], w_branch_a[l].astype(BF16),
            w_branch_b[l].astype(BF16), w_out[l].astype(BF16), norm_ffn_w[l][None, :],
            w_up[l].astype(BF16), conv_ffn_w[l], conv_ffn_b[l][None, :], w_down[l].astype(BF16),
            norm_final_w[None, :], l == depth - 1)
    return x
```

```python
import functools

import jax
import jax.numpy as jnp
from jax import lax
from jax.experimental import pallas as pl
from jax.experimental.pallas import tpu as pltpu

CHUNK = 64
EPS = 1e-6
GDN_HEADS = 4
GDN_DK = 128
GDN_DV = 128
GDN_CONV = 4
ATT_HEADS = 8
ATT_DH = 64
ATT_BAND = 9
REL_CLIP = 128
FFN_CONV = 3

KEY_A = GDN_HEADS * GDN_DK
VAL_A = GDN_HEADS * GDN_DV
WIDTH_B = ATT_HEADS * ATT_DH
CONV_A = 2 * KEY_A + VAL_A
LEAD = (ATT_BAND - 1) * CHUNK
BAND = ATT_BAND * CHUNK

V7X_SUBLANES = 8
V7X_LANES = 128
V7X_VMEM_LIMIT_BYTES = 56 * 1024 * 1024

BIAS_ROLL = -(-(BAND + CHUNK - 1) // V7X_LANES) * V7X_LANES

PROJ_ROWS = 256
PROJ_COLS = 256
GDN_CHUNKS = 4
ATT_ROWS = 512
FFN_ROWS = 256
FFN_COLS = 256
FFN_LOOKAHEAD = 8

F32 = jnp.float32
BF16 = jnp.bfloat16


def _dot(a, b):
    return jnp.dot(a.astype(BF16), b.astype(BF16), preferred_element_type=F32)


def _dot_nt(a, b):
    return lax.dot_general(a.astype(BF16), b.astype(BF16), (((1,), (1,)), ((), ())),
                           preferred_element_type=F32)


def _dot_tn(a, b):
    return lax.dot_general(a.astype(BF16), b.astype(BF16), (((0,), (0,)), ((), ())),
                           preferred_element_type=F32)


def _sigmoid(x):
    return 1.0 / (1.0 + jnp.exp(-x))


def _softplus(x):
    return jnp.maximum(x, 0.0) + jnp.log1p(jnp.exp(-jnp.abs(x)))


def _rms_scale(x):
    return lax.rsqrt(jnp.mean(x * x, axis=-1, keepdims=True) + EPS)


def _to_strands(x, perm_ref):
    tm, width = x.shape
    per_seg = tm // V7X_SUBLANES // V7X_SUBLANES
    for i in range(tm // V7X_SUBLANES):
        start = V7X_SUBLANES * V7X_SUBLANES * (i % per_seg) + i // per_seg
        for lb in range(width // V7X_LANES):
            perm_ref[lb, pl.ds(start, V7X_SUBLANES, stride=V7X_SUBLANES), :] = (
                x[i * V7X_SUBLANES:(i + 1) * V7X_SUBLANES, lb * V7X_LANES:(lb + 1) * V7X_LANES])
    return jnp.concatenate([perm_ref[lb] for lb in range(width // V7X_LANES)], axis=1)


def _from_strands(y, unp_ref):
    tm, width = y.shape
    seg = tm // V7X_SUBLANES
    pitch = unp_ref.shape[1] // V7X_SUBLANES
    for a in range(seg):
        for lb in range(width // V7X_LANES):
            unp_ref[lb, pl.ds(a, V7X_SUBLANES, stride=pitch), :] = (
                y[a * V7X_SUBLANES:(a + 1) * V7X_SUBLANES, lb * V7X_LANES:(lb + 1) * V7X_LANES])
    return jnp.concatenate(
        [jnp.concatenate([unp_ref[lb, s * pitch:s * pitch + seg, :] for s in range(V7X_SUBLANES)], axis=0)
         for lb in range(width // V7X_LANES)], axis=1)


def _strand_taps(u_ref, carry_ref, cols, n_taps):
    tm = u_ref.shape[0]
    back = n_taps - 1
    tail = u_ref[tm - back * V7X_SUBLANES:tm, cols]
    prev = carry_ref[:, cols]
    sub = lax.broadcasted_iota(jnp.int32, tail.shape, 0) % V7X_SUBLANES
    wrapped = jnp.where(sub == V7X_SUBLANES - 1, prev, tail)
    heads = [pltpu.roll(wrapped[r * V7X_SUBLANES:(r + 1) * V7X_SUBLANES], 1, 0) for r in range(back)]
    carry_ref[:, cols] = tail
    taps = [u_ref[:, cols]]
    for k in range(1, n_taps):
        taps.append(jnp.concatenate(heads[back - k:] + [u_ref[0:tm - k * V7X_SUBLANES, cols]], axis=0))
    return taps


def _const_spec(shape):
    zeros = (0,) * len(shape)
    return pl.BlockSpec(shape, lambda *_: zeros, pipeline_mode=pl.Buffered(1))


def _params(n_axes):
    return pltpu.CompilerParams(dimension_semantics=("arbitrary",) * n_axes,
                                vmem_limit_bytes=V7X_VMEM_LIMIT_BYTES)


def _proj_kernel(x_ref, nw_ref, wa_ref, wb_ref, wbg_ref, wbgt_ref, cw_ref, hpr_ref, hpc_ref,
                 a4_ref, qkvb_ref, gbc_ref, gbr_ref, cbuf_ref):
    tm = x_ref.shape[1]
    halo = V7X_SUBLANES
    x = x_ref[0]
    h = (x * _rms_scale(x) * nw_ref[...]).astype(BF16)

    @pl.when(pl.program_id(1) == 0)
    def _():
        cbuf_ref[0:halo, :] = jnp.zeros((halo, CONV_A), F32)

    def project_a(blk):
        cols = slice(blk * PROJ_COLS, (blk + 1) * PROJ_COLS)
        cbuf_ref[halo:, cols] = jnp.dot(h, wa_ref[:, cols], preferred_element_type=F32)

    def project_z(blk):
        cols = slice(CONV_A + blk * PROJ_COLS, CONV_A + (blk + 1) * PROJ_COLS)
        a4_ref[0, :, cols] = jnp.dot(h, wa_ref[:, cols], preferred_element_type=F32)

    def project_b(blk):
        cols = slice(blk * PROJ_COLS, (blk + 1) * PROJ_COLS)
        pb = jnp.dot(h, wb_ref[:, cols], preferred_element_type=F32)
        if (blk + 1) * PROJ_COLS <= WIDTH_B:
            pb = pb * (ATT_DH ** -0.5)
        qkvb_ref[0, :, cols] = pb.astype(BF16)

    def conv_head(blk):
        cols = slice(blk * V7X_LANES, (blk + 1) * V7X_LANES)
        conv = cw_ref[GDN_CONV - 1:GDN_CONV, cols] * cbuf_ref[halo:halo + tm, cols]
        for tap in range(GDN_CONV - 1):
            off = halo - (GDN_CONV - 1) + tap
            conv = conv + cw_ref[tap:tap + 1, cols] * cbuf_ref[off:off + tm, cols]
        cbuf_ref[0:halo, cols] = cbuf_ref[tm:tm + halo, cols]
        act = conv * _sigmoid(conv)
        if blk < 2 * GDN_HEADS:
            act = act * lax.rsqrt(jnp.sum(act * act, axis=-1, keepdims=True) + EPS)
        if blk < GDN_HEADS:
            act = act * (GDN_DK ** -0.5)
        a4_ref[0, :, cols] = act

    n_a = CONV_A // PROJ_COLS
    heads_per_blk = PROJ_COLS // V7X_LANES
    project_a(0)
    for blk in range(n_a):
        if blk + 1 < n_a:
            project_a(blk + 1)
        project_b(blk)
        if blk < VAL_A // PROJ_COLS:
            project_z(blk)
        for sub in range(heads_per_blk):
            conv_head(blk * heads_per_blk + sub)

    pc = jnp.dot(h, wbg_ref[...], preferred_element_type=F32)
    lane = lax.broadcasted_iota(jnp.int32, pc.shape, 1)
    gc = -jnp.exp(hpr_ref[0:1, :]) * _softplus(pc + hpr_ref[1:2, :])
    valc = jnp.where(lane < GDN_HEADS, _sigmoid(pc), gc)
    gbc_ref[0] = valc.reshape(tm // CHUNK, CHUNK, 2 * GDN_HEADS)

    pr = lax.dot_general(wbgt_ref[...], h, (((1,), (1,)), ((), ())),
                         preferred_element_type=F32)
    row = lax.broadcasted_iota(jnp.int32, pr.shape, 0)
    gr = -jnp.exp(hpc_ref[:, 0:1]) * _softplus(pr + hpc_ref[:, 1:2])
    valr = jnp.where(row < GDN_HEADS, _sigmoid(pr), gr)
    for c in range(tm // CHUNK):
        gbr_ref[0, c] = valr[:, c * CHUNK:(c + 1) * CHUNK]


def _project(x, nw, wa, wb, wbg, wbgt, cw, hpr, hpc):
    b, t, d = x.shape
    tm = PROJ_ROWS
    nc = tm // CHUNK
    return pl.pallas_call(
        _proj_kernel,
        grid=(b, t // tm),
        in_specs=[
            pl.BlockSpec((1, tm, d), lambda bi, i: (bi, i, 0)),
            _const_spec(nw.shape), _const_spec(wa.shape), _const_spec(wb.shape),
            _const_spec(wbg.shape), _const_spec(wbgt.shape), _const_spec(cw.shape),
            _const_spec(hpr.shape), _const_spec(hpc.shape),
        ],
        out_specs=[
            pl.BlockSpec((1, tm, CONV_A + VAL_A), lambda bi, i: (bi, i, 0)),
            pl.BlockSpec((1, tm, 3 * WIDTH_B), lambda bi, i: (bi, i, 0)),
            pl.BlockSpec((1, nc, CHUNK, 2 * GDN_HEADS), lambda bi, i: (bi, i, 0, 0)),
            pl.BlockSpec((1, nc, 2 * GDN_HEADS, CHUNK), lambda bi, i: (bi, i, 0, 0)),
        ],
        out_shape=[
            jax.ShapeDtypeStruct((b, t, CONV_A + VAL_A), F32),
            jax.ShapeDtypeStruct((b, t, 3 * WIDTH_B), BF16),
            jax.ShapeDtypeStruct((b, t // CHUNK, CHUNK, 2 * GDN_HEADS), F32),
            jax.ShapeDtypeStruct((b, t // CHUNK, 2 * GDN_HEADS, CHUNK), F32),
        ],
        scratch_shapes=[pltpu.VMEM((tm + V7X_SUBLANES, CONV_A), F32)],
        compiler_params=_params(2),
        name="in_proj",
    )(x, nw, wa, wb, wbg, wbgt, cw, hpr, hpc)


def _each(fn, *lists):
    return [fn(*args) for args in zip(*lists)]


def _unit_lower_inverse(a, ii, jj, between):
    same16 = (ii // 16) == (jj // 16)
    same32 = (ii // 32) == (jj // 32)
    eye = jnp.where(ii == jj, 1.0, 0.0).astype(F32)
    ad = _each(lambda x: jnp.where(same16, x, 0.0), a)
    inv = _each(lambda x: eye - x, ad)
    power = _each(_dot, ad, ad)
    between()
    for level in range(3):
        inv = _each(lambda i, pi: i + pi, inv, _each(_dot, power, inv))
        if level < 2:
            power = _each(_dot, power, power)
        between()
    for off in (_each(lambda x: jnp.where(same32 & jnp.logical_not(same16), x, 0.0), a),
                _each(lambda x: jnp.where(same32, 0.0, x), a)):
        t = _each(_dot, off, inv)
        between()
        inv = _each(lambda i, t_: i - t_, inv, _each(_dot, inv, t))
        between()
    return inv


def _exact_dot_01(a, b, ones_on_left):
    x = b if ones_on_left else a
    mask = (a if ones_on_left else b).astype(BF16)
    acc = None
    for _ in range(3):
        piece = x.astype(BF16)
        x = x - piece.astype(F32)
        term = (jnp.dot(mask, piece, preferred_element_type=F32) if ones_on_left
                else jnp.dot(piece, mask, preferred_element_type=F32))
        acc = term if acc is None else acc + term
    return acc


class _Interleaver:
    def __init__(self, pieces, period):
        self.pieces, self.period, self.calls = list(pieces), period, 0

    def __call__(self):
        self.calls += 1
        if self.calls % self.period == 0 and self.pieces:
            self.pieces.pop(0)()

    def flush(self):
        while self.pieces:
            self.pieces.pop(0)()


def _gdn_kernel(a4_ref, gbc_ref, gbr_ref, gnw_ref, oa_ref,
                s_ref, wq_ref, uv_ref, pm_ref, kd_ref, sz_ref, gl_ref):
    nb = a4_ref.shape[0]
    nc = gbc_ref.shape[1]
    step = pl.program_id(0)
    slot_w = step % 2
    slot_r = 1 - slot_w
    chains = [(bi, hd) for bi in range(nb) for hd in range(GDN_HEADS)]
    n_ch = len(chains)

    @pl.when(step == 0)
    def _():
        s_ref[...] = jnp.zeros(s_ref.shape, F32)
        for ref in (wq_ref, uv_ref, pm_ref, kd_ref, sz_ref, gl_ref):
            ref[1] = jnp.zeros(ref.shape[1:], ref.dtype)

    ii = lax.broadcasted_iota(jnp.int32, (CHUNK, CHUNK), 0)
    jj = lax.broadcasted_iota(jnp.int32, (CHUNK, CHUNK), 1)
    incl = ii >= jj
    strict = ii > jj
    tri_lo = jnp.where(incl, 1.0, 0.0).astype(F32)
    tri_up = jnp.where(ii <= jj, 1.0, 0.0).astype(F32)
    gnw = gnw_ref[...]

    def recurrence(c):
        rows = pl.ds(pl.multiple_of(c * CHUNK, CHUNK), CHUNK)
        at = [c * n_ch + ch for ch in range(n_ch)]
        st = {}

        def stage_ws():
            st["s"] = [s_ref[ch] for ch in range(n_ch)]
            st["ws"] = [_dot(wq_ref[slot_r, j], s_) for j, s_ in zip(at, st["s"])]

        def stage_update():
            u = [uv_ref[slot_r, j] - ws_[:CHUNK] for j, ws_ in zip(at, st["ws"])]
            pu = [_dot(pm_ref[slot_r, j], u_) for j, u_ in zip(at, u)]
            ku = [_dot_tn(kd_ref[slot_r, j], u_) for j, u_ in zip(at, u)]
            for ch, (j, s_, ku_) in enumerate(zip(at, st["s"], ku)):
                s_ref[ch] = jnp.exp(gl_ref[slot_r, j][0:1, :]) * s_ + ku_
            for (bi, hd), j, ws_, pu_ in zip(chains, at, st["ws"], pu):
                o = ws_[CHUNK:] + pu_
                o = o * _rms_scale(o) * gnw
                oa_ref[bi, rows, hd * GDN_DV:(hd + 1) * GDN_DV] = (o * sz_ref[slot_r, j]).astype(BF16)

        return [stage_ws, stage_update]

    def pair_step(pr, carry):
        pieces = _Interleaver(recurrence(2 * pr) + recurrence(2 * pr + 1), period=2)

        cs = [2 * pr, 2 * pr + 1]
        items = [(e, bi, hd) for e in range(2) for bi, hd in chains]
        rows = [pl.ds(pl.multiple_of(c * CHUNK, CHUNK), CHUNK) for c in cs]

        gcol = [[_exact_dot_01(tri_lo, gbc_ref[bi, c], ones_on_left=True) for bi in range(nb)] for c in cs]
        grow = [[_exact_dot_01(gbr_ref[bi, c], tri_up, ones_on_left=False) for bi in range(nb)] for c in cs]

        def load(col0):
            return [a4_ref[bi, rows[e], col0 + hd * GDN_DK:col0 + (hd + 1) * GDN_DK] for e, bi, hd in items]

        q, k, v = load(0), load(KEY_A), load(2 * KEY_A)
        g_c = [gcol[e][bi][:, GDN_HEADS + hd:GDN_HEADS + hd + 1] for e, bi, hd in items]
        g_r = [grow[e][bi][GDN_HEADS + hd:GDN_HEADS + hd + 1, :] for e, bi, hd in items]
        beta = [gbc_ref[bi, cs[e]][:, hd:hd + 1] for e, bi, hd in items]
        g_last = [x[CHUNK - 1:CHUNK, :] for x in g_c]
        dec = _each(lambda gc, gr: jnp.exp(jnp.where(incl, gc - gr, 0.0)), g_c, g_r)
        gam = _each(jnp.exp, g_c)

        kq = _each(lambda k_, q_: _dot_nt(jnp.concatenate([k_, q_], axis=0), k_), k, q)
        pieces()
        a_mat = _each(lambda b_, x, d_: jnp.where(strict, b_ * x[:CHUNK] * d_, 0.0), beta, kq, dec)
        p_mat = _each(lambda x, d_: jnp.where(incl, x[CHUNK:] * d_, 0.0), kq, dec)
        inv = _unit_lower_inverse(a_mat, ii, jj, pieces)
        rhs = _each(lambda b_, g_, k_, v_: jnp.concatenate([(b_ * g_) * k_, b_ * v_], axis=1),
                    beta, gam, k, v)
        sol = _each(_dot, inv, rhs)
        pieces.flush()

        z = load(CONV_A)
        for n, (e, bi, hd) in enumerate(items):
            j = cs[e] * n_ch + bi * GDN_HEADS + hd
            wq_ref[slot_w, j] = jnp.concatenate([sol[n][:, :GDN_DK], q[n] * gam[n]], axis=0).astype(BF16)
            uv_ref[slot_w, j] = sol[n][:, GDN_DK:]
            pm_ref[slot_w, j] = p_mat[n].astype(BF16)
            kd_ref[slot_w, j] = (k[n] * jnp.exp(g_last[n] - g_c[n])).astype(BF16)
            sz_ref[slot_w, j] = z[n] * _sigmoid(z[n])
            gl_ref[slot_w, j] = jnp.broadcast_to(g_last[n], gl_ref.shape[2:])
        return carry

    lax.fori_loop(0, nc // 2, pair_step, 0)


def _gated_delta(a4, gbc, gbr, gnw):
    b, t, _ = a4.shape
    nc = GDN_CHUNKS
    rows = nc * CHUNK
    n_tiles = t // rows
    per_tile = nc * b * GDN_HEADS

    def tile_in(i):
        return jnp.minimum(i, n_tiles - 1)

    def tile_out(i):
        return jnp.maximum(i - 1, 0)

    return pl.pallas_call(
        _gdn_kernel,
        grid=(n_tiles + 1,),
        in_specs=[
            pl.BlockSpec((b, rows, CONV_A + VAL_A), lambda i: (0, tile_in(i), 0)),
            pl.BlockSpec((b, nc, CHUNK, 2 * GDN_HEADS), lambda i: (0, tile_in(i), 0, 0)),
            pl.BlockSpec((b, nc, 2 * GDN_HEADS, CHUNK), lambda i: (0, tile_in(i), 0, 0)),
            _const_spec(gnw.shape),
        ],
        out_specs=pl.BlockSpec((b, rows, VAL_A), lambda i: (0, tile_out(i), 0)),
        out_shape=jax.ShapeDtypeStruct((b, t, VAL_A), BF16),
        scratch_shapes=[
            pltpu.VMEM((b * GDN_HEADS, GDN_DK, GDN_DV), F32),
            pltpu.VMEM((2, per_tile, 2 * CHUNK, GDN_DK), BF16),
            pltpu.VMEM((2, per_tile, CHUNK, GDN_DV), F32),
            pltpu.VMEM((2, per_tile, CHUNK, CHUNK), BF16),
            pltpu.VMEM((2, per_tile, CHUNK, GDN_DK), BF16),
            pltpu.VMEM((2, per_tile, CHUNK, GDN_DV), F32),
            pltpu.VMEM((2, per_tile, V7X_SUBLANES, V7X_LANES), F32),
        ],
        compiler_params=_params(1),
        name="gated_delta",
    )(a4, gbc, gbr, gnw)


def _attn_kernel(q_ref, k_ref, v_ref, rel_ref, ob_ref, kbuf_ref, vbuf_ref, bias_ref):
    tq = q_ref.shape[1]
    i = pl.program_id(1)

    @pl.when((i == 0) & (pl.program_id(0) == 0))
    def _():
        for hd in range(ATT_HEADS):
            rel = jnp.broadcast_to(rel_ref[hd:hd + 1, :], (CHUNK, BIAS_ROLL))
            rolled = pltpu.roll(rel, 0, 1, stride=1, stride_axis=0)
            bias_ref[hd // 2, (hd % 2) * CHUNK:(hd % 2 + 1) * CHUNK, :] = rolled[:, :BAND]

    @pl.when(i == 0)
    def _():
        kbuf_ref[0:LEAD, :] = jnp.zeros((LEAD, WIDTH_B), BF16)
        vbuf_ref[0:LEAD, :] = jnp.zeros((LEAD, WIDTH_B), BF16)

    @pl.when(i > 0)
    def _():
        kbuf_ref[0:LEAD, :] = kbuf_ref[tq:tq + LEAD, :]
        vbuf_ref[0:LEAD, :] = vbuf_ref[tq:tq + LEAD, :]

    kbuf_ref[LEAD:, :] = k_ref[0]
    vbuf_ref[LEAD:, :] = v_ref[0]

    lane = lax.broadcasted_iota(jnp.int32, (CHUNK, 2 * ATT_DH), 1)
    first = lane < ATT_DH
    key = lax.broadcasted_iota(jnp.int32, (1, BAND), 1)
    zero_q = jnp.zeros((CHUNK, 2 * ATT_DH), BF16)

    def chunk_step(c, carry):
        r0 = pl.multiple_of(c * CHUNK, CHUNK)
        valid = (i * tq - LEAD + c * CHUNK + key) >= 0
        pairs = list(range(ATT_HEADS // 2))
        lanes = [slice(pr * 2 * ATT_DH, (pr + 1) * 2 * ATT_DH) for pr in pairs]

        def stacked_q(ln):
            q = q_ref[0, pl.ds(r0, CHUNK), ln]
            return jnp.concatenate([jnp.where(first, q, zero_q), jnp.where(first, zero_q, q)], axis=0)

        s = [_dot_nt(stacked_q(ln), kbuf_ref[pl.ds(r0, BAND), ln]) for ln in lanes]
        s = [jnp.where(valid, x + bias_ref[pr], -jnp.inf) for x, pr in zip(s, pairs)]
        p = [jnp.exp(x - jnp.max(x, axis=-1, keepdims=True)) for x in s]
        denom = [jnp.sum(x, axis=-1, keepdims=True) for x in p]
        pv = [_dot(x, vbuf_ref[pl.ds(r0, BAND), ln]) for x, ln in zip(p, lanes)]
        for x, dn, ln in zip(pv, denom, lanes):
            x = x / dn
            ob_ref[0, pl.ds(r0, CHUNK), ln] = jnp.where(first, x[:CHUNK], x[CHUNK:]).astype(BF16)
        return carry

    lax.fori_loop(0, tq // CHUNK, chunk_step, 0)


def _band_attention(qkvb, rel):
    b, t, _ = qkvb.shape
    tq = ATT_ROWS
    return pl.pallas_call(
        _attn_kernel,
        grid=(b, t // tq),
        in_specs=[
            pl.BlockSpec((1, tq, WIDTH_B), lambda bi, i: (bi, i, 0)),
            pl.BlockSpec((1, tq, WIDTH_B), lambda bi, i: (bi, i, 1)),
            pl.BlockSpec((1, tq, WIDTH_B), lambda bi, i: (bi, i, 2)),
            _const_spec(rel.shape),
        ],
        out_specs=pl.BlockSpec((1, tq, WIDTH_B), lambda bi, i: (bi, i, 0)),
        out_shape=jax.ShapeDtypeStruct((b, t, WIDTH_B), BF16),
        scratch_shapes=[pltpu.VMEM((LEAD + tq, WIDTH_B), BF16),
                        pltpu.VMEM((LEAD + tq, WIDTH_B), BF16),
                        pltpu.VMEM((ATT_HEADS // 2, 2 * CHUNK, BAND), F32)],
        compiler_params=_params(2),
        name="band_attention",
    )(qkvb, qkvb, qkvb, rel)


def _ffn_kernel(final_norm, x_ref, oa_ref, ob_ref, nmw_ref, wg_ref, wa_ref, wb_ref, wo_ref,
                nfw_ref, wup_ref, cfw_ref, cfb_ref, wdn_ref, nlw_ref, out_ref,
                ubuf_ref, carry_ref, perm_ref, unp_ref):
    d = x_ref.shape[2]
    dff = wdn_ref.shape[0]
    x = x_ref[0]

    h = (x * _rms_scale(x) * nmw_ref[...]).astype(BF16)
    gates = jnp.dot(h, wg_ref[...], preferred_element_type=F32)
    ya = jnp.dot(oa_ref[0], wa_ref[...], preferred_element_type=F32)
    yb = jnp.dot(ob_ref[0], wb_ref[...], preferred_element_type=F32)
    mix = _sigmoid(gates[:, :d]) * ya + _sigmoid(gates[:, d:]) * yb
    x1 = x + _dot(mix, wo_ref[...])

    @pl.when(pl.program_id(1) == 0)
    def _():
        carry_ref[...] = jnp.zeros(carry_ref.shape, F32)

    h2 = _to_strands(x1 * _rms_scale(x1) * nfw_ref[...], perm_ref).astype(BF16)

    def gate_up_cols(blk):
        return [slice(half * dff + blk * FFN_COLS, half * dff + (blk + 1) * FFN_COLS) for half in range(2)]

    def project_up(blk):
        for cols in gate_up_cols(blk):
            ubuf_ref[:, cols] = jnp.dot(h2, wup_ref[:, cols], preferred_element_type=F32)

    def conv_block(cols):
        acc = cfb_ref[:, cols]
        for k, tap in enumerate(_strand_taps(ubuf_ref, carry_ref, cols, FFN_CONV)):
            acc = acc + cfw_ref[FFN_CONV - 1 - k:FFN_CONV - k, cols] * tap
        return acc

    n_blk = dff // FFN_COLS
    for blk in range(min(FFN_LOOKAHEAD, n_blk)):
        project_up(blk)
    y = None
    for blk in range(n_blk):
        if blk + FFN_LOOKAHEAD < n_blk:
            project_up(blk + FFN_LOOKAHEAD)
        gate, up = [conv_block(cols) for cols in gate_up_cols(blk)]
        act = (gate * _sigmoid(gate) * up).astype(BF16)
        part = jnp.dot(act, wdn_ref[blk * FFN_COLS:(blk + 1) * FFN_COLS, :], preferred_element_type=F32)
        y = part if y is None else y + part
    x2 = x1 + _from_strands(y, unp_ref)
    if final_norm:
        x2 = x2 * _rms_scale(x2) * nlw_ref[...]
    out_ref[0] = x2


def _merge_ffn(x, oa, ob, nmw, wg, wa, wb, wo, nfw, wup, cfw, cfb, wdn, nlw, final_norm):
    b, t, d = x.shape
    tm = FFN_ROWS
    dff = wdn.shape[0]
    consts = (nmw, wg, wa, wb, wo, nfw, wup, cfw, cfb, wdn, nlw)
    return pl.pallas_call(
        functools.partial(_ffn_kernel, final_norm),
        grid=(b, t // tm),
        in_specs=[
            pl.BlockSpec((1, tm, d), lambda bi, i: (bi, i, 0)),
            pl.BlockSpec((1, tm, VAL_A), lambda bi, i: (bi, i, 0)),
            pl.BlockSpec((1, tm, WIDTH_B), lambda bi, i: (bi, i, 0)),
        ] + [_const_spec(c.shape) for c in consts],
        out_specs=pl.BlockSpec((1, tm, d), lambda bi, i: (bi, i, 0)),
        out_shape=jax.ShapeDtypeStruct((b, t, d), F32),
        scratch_shapes=[
            pltpu.VMEM((tm, 2 * dff), F32),
            pltpu.VMEM(((FFN_CONV - 1) * V7X_SUBLANES, 2 * dff), F32),
            pltpu.VMEM((d // V7X_LANES, tm, V7X_LANES), F32),
            pltpu.VMEM((d // V7X_LANES, tm + V7X_SUBLANES * V7X_SUBLANES, V7X_LANES), F32),
        ],
        compiler_params=_params(2),
        name="merge_ffn",
    )(x, oa, ob, *consts)


def _rel_bias_row(rel_table):
    n = jnp.arange(BIAS_ROLL)
    dist = jnp.where(n < BAND, jnp.clip(LEAD - n, -REL_CLIP, REL_CLIP), REL_CLIP)
    return rel_table.astype(F32)[:, dist + REL_CLIP]


def kernel(x, norm_mix_w, w_in, conv_qkv_w, a_log, dt_bias, gdn_norm_w, w_branch_a, w_branch_b,
           rel_bias, w_out, norm_ffn_w, w_up, conv_ffn_w, conv_ffn_b, w_down, norm_final_w):
    depth = w_in.shape[0]
    d = x.shape[-1]
    o_z = CONV_A + VAL_A
    o_bg = o_z + 2 * GDN_HEADS
    o_b = o_bg + 3 * WIDTH_B
    zeros_h = jnp.zeros((GDN_HEADS,), F32)
    for l in range(depth):
        w = w_in[l].astype(BF16)
        wbg = w[:, o_z:o_bg]
        hpr = jnp.stack([jnp.concatenate([zeros_h, a_log[l].astype(F32)]),
                         jnp.concatenate([zeros_h, dt_bias[l].astype(F32)])])
        a4, qkvb, gbc, gbr = _project(
            x, norm_mix_w[l][None, :], w[:, :o_z], w[:, o_bg:o_b], wbg, wbg.T,
            conv_qkv_w[l], hpr, hpr.T)
        oa = _gated_delta(a4, gbc, gbr, gdn_norm_w[l][None, :])
        ob = _band_attention(qkvb, _rel_bias_row(rel_bias[l]))
        x = _merge_ffn(
            x, oa, ob, norm_mix_w[l][None, :], w[:, o_b:], w_branch_a[l].astype(BF16),
            w_branch_b[l].astype(BF16), w_out[l].astype(BF16), norm_ffn_w[l][None, :],
            w_up[l].astype(BF16), conv_ffn_w[l], conv_ffn_b[l][None, :], w_down[l].astype(BF16),
            norm_final_w[None, :], l == depth - 1)
    return x
```

```python
import functools

import jax
import jax.numpy as jnp
from jax import lax
from jax.experimental import pallas as pl
from jax.experimental.pallas import tpu as pltpu

CHUNK = 64
EPS = 1e-6
GDN_HEADS = 4
GDN_DK = 128
GDN_DV = 128
GDN_CONV = 4
ATT_HEADS = 8
ATT_DH = 64
ATT_BAND = 9
REL_CLIP = 128
FFN_CONV = 3

KEY_A = GDN_HEADS * GDN_DK
VAL_A = GDN_HEADS * GDN_DV
WIDTH_B = ATT_HEADS * ATT_DH
CONV_A = 2 * KEY_A + VAL_A
LEAD = (ATT_BAND - 1) * CHUNK
BAND = ATT_BAND * CHUNK

V7X_SUBLANES = 8
V7X_LANES = 128
V7X_VMEM_LIMIT_BYTES = 56 * 1024 * 1024

BIAS_ROLL = -(-(BAND + CHUNK - 1) // V7X_LANES) * V7X_LANES

PROJ_ROWS = 256
PROJ_COLS = 256
GDN_CHUNKS = 4
FFN_ROWS = 256
FFN_COLS = 256
FFN_LOOKAHEAD = 8

F32 = jnp.float32
BF16 = jnp.bfloat16


def _dot(a, b):
    return jnp.dot(a.astype(BF16), b.astype(BF16), preferred_element_type=F32)


def _dot_nt(a, b):
    return lax.dot_general(a.astype(BF16), b.astype(BF16), (((1,), (1,)), ((), ())),
                           preferred_element_type=F32)


def _dot_tn(a, b):
    return lax.dot_general(a.astype(BF16), b.astype(BF16), (((0,), (0,)), ((), ())),
                           preferred_element_type=F32)


def _sigmoid(x):
    return 1.0 / (1.0 + jnp.exp(-x))


def _softplus(x):
    return jnp.maximum(x, 0.0) + jnp.log1p(jnp.exp(-jnp.abs(x)))


def _rms_scale(x):
    return lax.rsqrt(jnp.mean(x * x, axis=-1, keepdims=True) + EPS)


def _each(fn, *lists):
    return [fn(*args) for args in zip(*lists)]


class _Interleaver:
    def __init__(self, pieces, period):
        self.pieces, self.period, self.calls = list(pieces), period, 0

    def __call__(self):
        self.calls += 1
        if self.calls % self.period == 0 and self.pieces:
            self.pieces.pop(0)()

    def flush(self):
        while self.pieces:
            self.pieces.pop(0)()


def _const_spec(shape):
    zeros = (0,) * len(shape)
    return pl.BlockSpec(shape, lambda *_: zeros, pipeline_mode=pl.Buffered(1))


def _params(n_axes):
    return pltpu.CompilerParams(dimension_semantics=("arbitrary",) * n_axes,
                                vmem_limit_bytes=V7X_VMEM_LIMIT_BYTES)


def _proj_kernel(x_ref, nw_ref, wa_ref, wb_ref, wbg_ref, wbgt_ref, cw_ref, hpr_ref, hpc_ref,
                 a4_ref, qkvb_ref, gbc_ref, gbr_ref, cbuf_ref):
    tm = x_ref.shape[1]
    halo = V7X_SUBLANES
    x = x_ref[0]
    h = (x * _rms_scale(x) * nw_ref[...]).astype(BF16)

    pc = jnp.dot(h, wbg_ref[...], preferred_element_type=F32)
    lane = lax.broadcasted_iota(jnp.int32, pc.shape, 1)
    gc = -jnp.exp(hpr_ref[0:1, :]) * _softplus(pc + hpr_ref[1:2, :])
    valc = jnp.where(lane < GDN_HEADS, _sigmoid(pc), gc)
    gbc_ref[0] = valc.reshape(tm // CHUNK, CHUNK, 2 * GDN_HEADS)

    pr = lax.dot_general(wbgt_ref[...], h, (((1,), (1,)), ((), ())),
                         preferred_element_type=F32)
    row = lax.broadcasted_iota(jnp.int32, pr.shape, 0)
    gr = -jnp.exp(hpc_ref[:, 0:1]) * _softplus(pr + hpc_ref[:, 1:2])
    valr = jnp.where(row < GDN_HEADS, _sigmoid(pr), gr)
    for c in range(tm // CHUNK):
        gbr_ref[0, c] = valr[:, c * CHUNK:(c + 1) * CHUNK]

    @pl.when(pl.program_id(1) == 0)
    def _():
        cbuf_ref[0:halo, :] = jnp.zeros((halo, CONV_A), F32)

    def project_a(blk):
        cols = slice(blk * PROJ_COLS, (blk + 1) * PROJ_COLS)
        cbuf_ref[halo:, cols] = jnp.dot(h, wa_ref[:, cols], preferred_element_type=F32)

    def project_z(blk):
        cols = slice(CONV_A + blk * PROJ_COLS, CONV_A + (blk + 1) * PROJ_COLS)
        a4_ref[0, :, cols] = jnp.dot(h, wa_ref[:, cols], preferred_element_type=F32)

    def project_b(blk):
        cols = slice(blk * PROJ_COLS, (blk + 1) * PROJ_COLS)
        pb = jnp.dot(h, wb_ref[:, cols], preferred_element_type=F32)
        if (blk + 1) * PROJ_COLS <= WIDTH_B:
            pb = pb * (ATT_DH ** -0.5)
        qkvb_ref[0, :, cols] = pb.astype(BF16)

    def conv_head(blk):
        cols = slice(blk * V7X_LANES, (blk + 1) * V7X_LANES)
        conv = cw_ref[GDN_CONV - 1:GDN_CONV, cols] * cbuf_ref[halo:halo + tm, cols]
        for tap in range(GDN_CONV - 1):
            off = halo - (GDN_CONV - 1) + tap
            conv = conv + cw_ref[tap:tap + 1, cols] * cbuf_ref[off:off + tm, cols]
        cbuf_ref[0:halo, cols] = cbuf_ref[tm:tm + halo, cols]
        act = conv * _sigmoid(conv)
        if blk < 2 * GDN_HEADS:
            act = act * lax.rsqrt(jnp.sum(act * act, axis=-1, keepdims=True) + EPS)
        if blk < GDN_HEADS:
            act = act * (GDN_DK ** -0.5)
        a4_ref[0, :, cols] = act

    n_a = CONV_A // PROJ_COLS
    heads_per_blk = PROJ_COLS // V7X_LANES
    project_a(0)
    for blk in range(n_a):
        if blk + 1 < n_a:
            project_a(blk + 1)
        project_b(blk)
        if blk < VAL_A // PROJ_COLS:
            project_z(blk)
        for sub in range(heads_per_blk):
            conv_head(blk * heads_per_blk + sub)


def _project(x, nw, wa, wb, wbg, wbgt, cw, hpr, hpc):
    b, t, d = x.shape
    tm = PROJ_ROWS
    nc = tm // CHUNK
    return pl.pallas_call(
        _proj_kernel,
        grid=(b, t // tm),
        in_specs=[
            pl.BlockSpec((1, tm, d), lambda bi, i: (bi, i, 0)),
            _const_spec(nw.shape), _const_spec(wa.shape), _const_spec(wb.shape),
            _const_spec(wbg.shape), _const_spec(wbgt.shape), _const_spec(cw.shape),
            _const_spec(hpr.shape), _const_spec(hpc.shape),
        ],
        out_specs=[
            pl.BlockSpec((1, tm, CONV_A + VAL_A), lambda bi, i: (bi, i, 0)),
            pl.BlockSpec((1, tm, 3 * WIDTH_B), lambda bi, i: (bi, i, 0)),
            pl.BlockSpec((1, nc, CHUNK, 2 * GDN_HEADS), lambda bi, i: (bi, i, 0, 0)),
            pl.BlockSpec((1, nc, 2 * GDN_HEADS, CHUNK), lambda bi, i: (bi, i, 0, 0)),
        ],
        out_shape=[
            jax.ShapeDtypeStruct((b, t, CONV_A + VAL_A), F32),
            jax.ShapeDtypeStruct((b, t, 3 * WIDTH_B), BF16),
            jax.ShapeDtypeStruct((b, t // CHUNK, CHUNK, 2 * GDN_HEADS), F32),
            jax.ShapeDtypeStruct((b, t // CHUNK, 2 * GDN_HEADS, CHUNK), F32),
        ],
        scratch_shapes=[pltpu.VMEM((tm + V7X_SUBLANES, CONV_A), F32)],
        compiler_params=_params(2),
        name="in_proj",
    )(x, nw, wa, wb, wbg, wbgt, cw, hpr, hpc)


def _unit_lower_inverse(a, ii, jj, between):
    same16 = (ii // 16) == (jj // 16)
    same32 = (ii // 32) == (jj // 32)
    eye = jnp.where(ii == jj, 1.0, 0.0).astype(F32)
    ad = _each(lambda x: jnp.where(same16, x, 0.0), a)
    inv = _each(lambda x: eye - x, ad)
    power = _each(_dot, ad, ad)
    between()
    for level in range(3):
        inv = _each(lambda i, pi: i + pi, inv, _each(_dot, power, inv))
        if level < 2:
            power = _each(_dot, power, power)
        between()
    for off in (_each(lambda x: jnp.where(same32 & jnp.logical_not(same16), x, 0.0), a),
                _each(lambda x: jnp.where(same32, 0.0, x), a)):
        t = _each(_dot, off, inv)
        between()
        inv = _each(lambda i, t_: i - t_, inv, _each(_dot, inv, t))
        between()
    return inv


def _exact_dot_01(a, b, ones_on_left):
    x = b if ones_on_left else a
    mask = (a if ones_on_left else b).astype(BF16)
    acc = None
    for _ in range(3):
        piece = x.astype(BF16)
        x = x - piece.astype(F32)
        term = (jnp.dot(mask, piece, preferred_element_type=F32) if ones_on_left
                else jnp.dot(piece, mask, preferred_element_type=F32))
        acc = term if acc is None else acc + term
    return acc


def _gdn_kernel(a4_ref, gbc_ref, gbr_ref, gnw_ref, oa_ref,
                s_ref, wq_ref, uv_ref, pm_ref, kd_ref, sz_ref, gl_ref):
    nb = a4_ref.shape[0]
    nc = gbc_ref.shape[1]
    step = pl.program_id(0)
    slot_w = step % 2
    slot_r = 1 - slot_w
    chains = [(bi, hd) for bi in range(nb) for hd in range(GDN_HEADS)]
    n_ch = len(chains)

    @pl.when(step == 0)
    def _():
        s_ref[...] = jnp.zeros(s_ref.shape, F32)
        for ref in (wq_ref, uv_ref, pm_ref, kd_ref, sz_ref, gl_ref):
            ref[1] = jnp.zeros(ref.shape[1:], ref.dtype)

    ii = lax.broadcasted_iota(jnp.int32, (CHUNK, CHUNK), 0)
    jj = lax.broadcasted_iota(jnp.int32, (CHUNK, CHUNK), 1)
    incl = ii >= jj
    strict = ii > jj
    tri_lo = jnp.where(incl, 1.0, 0.0).astype(F32)
    tri_up = jnp.where(ii <= jj, 1.0, 0.0).astype(F32)
    gnw = gnw_ref[...]

    def recurrence(c):
        rows = pl.ds(pl.multiple_of(c * CHUNK, CHUNK), CHUNK)
        at = [c * n_ch + ch for ch in range(n_ch)]
        st = {}

        def stage_ws():
            st["s"] = [s_ref[ch] for ch in range(n_ch)]
            st["ws"] = [_dot(wq_ref[slot_r, j], s_) for j, s_ in zip(at, st["s"])]

        def stage_update():
            u = [uv_ref[slot_r, j] - ws_[:CHUNK] for j, ws_ in zip(at, st["ws"])]
            pu = [_dot(pm_ref[slot_r, j], u_) for j, u_ in zip(at, u)]
            ku = [_dot_tn(kd_ref[slot_r, j], u_) for j, u_ in zip(at, u)]
            for ch, (j, s_, ku_) in enumerate(zip(at, st["s"], ku)):
                s_ref[ch] = jnp.exp(gl_ref[slot_r, j][0:1, :]) * s_ + ku_
            for (bi, hd), j, ws_, pu_ in zip(chains, at, st["ws"], pu):
                o = ws_[CHUNK:] + pu_
                o = o * _rms_scale(o) * gnw
                oa_ref[bi, rows, hd * GDN_DV:(hd + 1) * GDN_DV] = (o * sz_ref[slot_r, j]).astype(BF16)

        return [stage_ws, stage_update]

    def pair_step(pr, carry):
        pieces = _Interleaver(recurrence(2 * pr) + recurrence(2 * pr + 1), period=2)

        cs = [2 * pr, 2 * pr + 1]
        items = [(e, bi, hd) for e in range(2) for bi, hd in chains]
        rows = [pl.ds(pl.multiple_of(c * CHUNK, CHUNK), CHUNK) for c in cs]

        gcol = [[_exact_dot_01(tri_lo, gbc_ref[bi, c], ones_on_left=True) for bi in range(nb)] for c in cs]
        grow = [[_exact_dot_01(gbr_ref[bi, c], tri_up, ones_on_left=False) for bi in range(nb)] for c in cs]

        def load(col0):
            return [a4_ref[bi, rows[e], col0 + hd * GDN_DK:col0 + (hd + 1) * GDN_DK] for e, bi, hd in items]

        q, k, v = load(0), load(KEY_A), load(2 * KEY_A)
        g_c = [gcol[e][bi][:, GDN_HEADS + hd:GDN_HEADS + hd + 1] for e, bi, hd in items]
        g_r = [grow[e][bi][GDN_HEADS + hd:GDN_HEADS + hd + 1, :] for e, bi, hd in items]
        beta = [gbc_ref[bi, cs[e]][:, hd:hd + 1] for e, bi, hd in items]
        g_last = [x[CHUNK - 1:CHUNK, :] for x in g_c]
        dec = _each(lambda gc, gr: jnp.exp(jnp.where(incl, gc - gr, 0.0)), g_c, g_r)
        gam = _each(jnp.exp, g_c)

        kq = _each(lambda k_, q_: _dot_nt(jnp.concatenate([k_, q_], axis=0), k_), k, q)
        pieces()
        a_mat = _each(lambda b_, x, d_: jnp.where(strict, b_ * x[:CHUNK] * d_, 0.0), beta, kq, dec)
        p_mat = _each(lambda x, d_: jnp.where(incl, x[CHUNK:] * d_, 0.0), kq, dec)
        inv = _unit_lower_inverse(a_mat, ii, jj, pieces)
        rhs = _each(lambda b_, g_, k_, v_: jnp.concatenate([(b_ * g_) * k_, b_ * v_], axis=1),
                    beta, gam, k, v)
        sol = _each(_dot, inv, rhs)
        pieces.flush()

        z = load(CONV_A)
        for n, (e, bi, hd) in enumerate(items):
            j = cs[e] * n_ch + bi * GDN_HEADS + hd
            wq_ref[slot_w, j] = jnp.concatenate([sol[n][:, :GDN_DK], q[n] * gam[n]], axis=0).astype(BF16)
            uv_ref[slot_w, j] = sol[n][:, GDN_DK:]
            pm_ref[slot_w, j] = p_mat[n].astype(BF16)
            kd_ref[slot_w, j] = (k[n] * jnp.exp(g_last[n] - g_c[n])).astype(BF16)
            sz_ref[slot_w, j] = z[n] * _sigmoid(z[n])
            gl_ref[slot_w, j] = jnp.broadcast_to(g_last[n], gl_ref.shape[2:])
        return carry

    lax.fori_loop(0, nc // 2, pair_step, 0)


def _gated_delta(a4, gbc, gbr, gnw):
    b, t, _ = a4.shape
    nc = GDN_CHUNKS
    rows = nc * CHUNK
    n_tiles = t // rows
    per_tile = nc * b * GDN_HEADS

    def tile_in(i):
        return jnp.minimum(i, n_tiles - 1)

    def tile_out(i):
        return jnp.maximum(i - 1, 0)

    return pl.pallas_call(
        _gdn_kernel,
        grid=(n_tiles + 1,),
        in_specs=[
            pl.BlockSpec((b, rows, CONV_A + VAL_A), lambda i: (0, tile_in(i), 0)),
            pl.BlockSpec((b, nc, CHUNK, 2 * GDN_HEADS), lambda i: (0, tile_in(i), 0, 0)),
            pl.BlockSpec((b, nc, 2 * GDN_HEADS, CHUNK), lambda i: (0, tile_in(i), 0, 0)),
            _const_spec(gnw.shape),
        ],
        out_specs=pl.BlockSpec((b, rows, VAL_A), lambda i: (0, tile_out(i), 0)),
        out_shape=jax.ShapeDtypeStruct((b, t, VAL_A), BF16),
        scratch_shapes=[
            pltpu.VMEM((b * GDN_HEADS, GDN_DK, GDN_DV), F32),
            pltpu.VMEM((2, per_tile, 2 * CHUNK, GDN_DK), BF16),
            pltpu.VMEM((2, per_tile, CHUNK, GDN_DV), F32),
            pltpu.VMEM((2, per_tile, CHUNK, CHUNK), BF16),
            pltpu.VMEM((2, per_tile, CHUNK, GDN_DK), BF16),
            pltpu.VMEM((2, per_tile, CHUNK, GDN_DV), F32),
            pltpu.VMEM((2, per_tile, V7X_SUBLANES, V7X_LANES), F32),
        ],
        compiler_params=_params(1),
        name="gated_delta",
    )(a4, gbc, gbr, gnw)


def _attention_prepare(k_ref, v_ref, rel_ref, kbuf_ref, vbuf_ref, bias_ref):
    tq = k_ref.shape[1]
    i = pl.program_id(1)

    @pl.when((i == 0) & (pl.program_id(0) == 0))
    def _():
        for hd in range(ATT_HEADS):
            rel = jnp.broadcast_to(rel_ref[hd:hd + 1, :], (CHUNK, BIAS_ROLL))
            rolled = pltpu.roll(rel, 0, 1, stride=1, stride_axis=0)
            bias_ref[hd // 2, (hd % 2) * CHUNK:(hd % 2 + 1) * CHUNK, :] = rolled[:, :BAND]

    @pl.when(i == 0)
    def _():
        kbuf_ref[0:LEAD, :] = jnp.zeros((LEAD, WIDTH_B), BF16)
        vbuf_ref[0:LEAD, :] = jnp.zeros((LEAD, WIDTH_B), BF16)

    @pl.when(i > 0)
    def _():
        kbuf_ref[0:LEAD, :] = kbuf_ref[tq:tq + LEAD, :]
        vbuf_ref[0:LEAD, :] = vbuf_ref[tq:tq + LEAD, :]

    kbuf_ref[LEAD:, :] = k_ref[0]
    vbuf_ref[LEAD:, :] = v_ref[0]


def _attention_chunk(c, q_ref, kbuf_ref, vbuf_ref, bias_ref, ob_ref):
    tq = q_ref.shape[1]
    i = pl.program_id(1)
    rows = slice(c * CHUNK, (c + 1) * CHUNK)
    band = slice(c * CHUNK, c * CHUNK + BAND)
    pairs = list(range(ATT_HEADS // 2))
    lanes = [slice(pr * 2 * ATT_DH, (pr + 1) * 2 * ATT_DH) for pr in pairs]
    first = lax.broadcasted_iota(jnp.int32, (CHUNK, 2 * ATT_DH), 1) < ATT_DH
    st = {}

    def scores():
        key = lax.broadcasted_iota(jnp.int32, (1, BAND), 1)
        valid = (i * tq - LEAD + c * CHUNK + key) >= 0
        zero_q = jnp.zeros((CHUNK, 2 * ATT_DH), BF16)

        def stacked_q(ln):
            q = q_ref[0, rows, ln]
            return jnp.concatenate([jnp.where(first, q, zero_q), jnp.where(first, zero_q, q)], axis=0)

        s = [_dot_nt(stacked_q(ln), kbuf_ref[band, ln]) for ln in lanes]
        s = [jnp.where(valid, x + bias_ref[pr], -jnp.inf) for x, pr in zip(s, pairs)]
        p = [jnp.exp(x - jnp.max(x, axis=-1, keepdims=True)) for x in s]
        st["denom"] = [jnp.sum(x, axis=-1, keepdims=True) for x in p]
        st["p"] = [x.astype(BF16) for x in p]

    def values():
        pv = [_dot(x, vbuf_ref[band, ln]) for x, ln in zip(st["p"], lanes)]
        for x, dn, ln in zip(pv, st["denom"], lanes):
            x = x / dn
            ob_ref[rows, ln] = jnp.where(first, x[:CHUNK], x[CHUNK:]).astype(BF16)

    return [scores, values]


def _ffn_kernel(final_norm, x_ref, oa_ref, q_ref, k_ref, v_ref, rel_ref, nmw_ref, wg_ref, wa_ref,
                wb_ref, wo_ref, nfw_ref, wup_ref, cfw_ref, cfb_ref, wdn_ref, nlw_ref, out_ref,
                ubuf_ref, kbuf_ref, vbuf_ref, bias_ref, ob_ref):
    tm = x_ref.shape[1]
    d = x_ref.shape[2]
    dff = wdn_ref.shape[0]
    halo = V7X_SUBLANES
    x = x_ref[0]

    _attention_prepare(k_ref, v_ref, rel_ref, kbuf_ref, vbuf_ref, bias_ref)
    attention = _Interleaver(
        [piece for c in range(tm // CHUNK)
         for piece in _attention_chunk(c, q_ref, kbuf_ref, vbuf_ref, bias_ref, ob_ref)], period=1)

    h = (x * _rms_scale(x) * nmw_ref[...]).astype(BF16)
    gates = []
    for blk in range(2 * d // FFN_COLS):
        gates.append(jnp.dot(h, wg_ref[:, blk * FFN_COLS:(blk + 1) * FFN_COLS], preferred_element_type=F32))
        attention()
    ya = jnp.dot(oa_ref[0], wa_ref[...], preferred_element_type=F32)
    attention.flush()
    gates = jnp.concatenate(gates, axis=1)
    yb = jnp.dot(ob_ref[...], wb_ref[...], preferred_element_type=F32)
    mix = _sigmoid(gates[:, :d]) * ya + _sigmoid(gates[:, d:]) * yb
    x1 = x + _dot(mix, wo_ref[...])

    @pl.when(pl.program_id(1) == 0)
    def _():
        ubuf_ref[0:halo, :] = jnp.zeros((halo, 2 * dff), F32)

    h2 = (x1 * _rms_scale(x1) * nfw_ref[...]).astype(BF16)

    def gate_up_cols(blk):
        return [slice(half * dff + blk * FFN_COLS, half * dff + (blk + 1) * FFN_COLS) for half in range(2)]

    def project_up(blk):
        for cols in gate_up_cols(blk):
            ubuf_ref[halo:, cols] = jnp.dot(h2, wup_ref[:, cols], preferred_element_type=F32)

    def conv_block(cols):
        acc = cfb_ref[:, cols] + cfw_ref[FFN_CONV - 1:FFN_CONV, cols] * ubuf_ref[halo:halo + tm, cols]
        for tap in range(FFN_CONV - 1):
            off = halo - (FFN_CONV - 1) + tap
            acc = acc + cfw_ref[tap:tap + 1, cols] * ubuf_ref[off:off + tm, cols]
        ubuf_ref[0:halo, cols] = ubuf_ref[tm:tm + halo, cols]
        return acc

    n_blk = dff // FFN_COLS
    for blk in range(min(FFN_LOOKAHEAD, n_blk)):
        project_up(blk)
    x2 = x1
    for blk in range(n_blk):
        if blk + FFN_LOOKAHEAD < n_blk:
            project_up(blk + FFN_LOOKAHEAD)
        gate, up = [conv_block(cols) for cols in gate_up_cols(blk)]
        act = (gate * _sigmoid(gate) * up).astype(BF16)
        x2 = x2 + jnp.dot(act, wdn_ref[blk * FFN_COLS:(blk + 1) * FFN_COLS, :], preferred_element_type=F32)
    if final_norm:
        x2 = x2 * _rms_scale(x2) * nlw_ref[...]
    out_ref[0] = x2


def _merge_ffn(x, oa, qkvb, rel, nmw, wg, wa, wb, wo, nfw, wup, cfw, cfb, wdn, nlw, final_norm):
    b, t, d = x.shape
    tm = FFN_ROWS
    dff = wdn.shape[0]
    consts = (rel, nmw, wg, wa, wb, wo, nfw, wup, cfw, cfb, wdn, nlw)
    return pl.pallas_call(
        functools.partial(_ffn_kernel, final_norm),
        grid=(b, t // tm),
        in_specs=[
            pl.BlockSpec((1, tm, d), lambda bi, i: (bi, i, 0)),
            pl.BlockSpec((1, tm, VAL_A), lambda bi, i: (bi, i, 0)),
            pl.BlockSpec((1, tm, WIDTH_B), lambda bi, i: (bi, i, 0)),
            pl.BlockSpec((1, tm, WIDTH_B), lambda bi, i: (bi, i, 1)),
            pl.BlockSpec((1, tm, WIDTH_B), lambda bi, i: (bi, i, 2)),
        ] + [_const_spec(c.shape) for c in consts],
        out_specs=pl.BlockSpec((1, tm, d), lambda bi, i: (bi, i, 0)),
        out_shape=jax.ShapeDtypeStruct((b, t, d), F32),
        scratch_shapes=[
            pltpu.VMEM((tm + V7X_SUBLANES, 2 * dff), F32),
            pltpu.VMEM((LEAD + tm, WIDTH_B), BF16),
            pltpu.VMEM((LEAD + tm, WIDTH_B), BF16),
            pltpu.VMEM((ATT_HEADS // 2, 2 * CHUNK, BAND), F32),
            pltpu.VMEM((tm, WIDTH_B), BF16),
        ],
        compiler_params=_params(2),
        name="merge_ffn",
    )(x, oa, qkvb, qkvb, qkvb, *consts)


def _rel_bias_row(rel_table):
    n = jnp.arange(BIAS_ROLL)
    dist = jnp.where(n < BAND, jnp.clip(LEAD - n, -REL_CLIP, REL_CLIP), REL_CLIP)
    return rel_table.astype(F32)[:, dist + REL_CLIP]


def kernel(x, norm_mix_w, w_in, conv_qkv_w, a_log, dt_bias, gdn_norm_w, w_branch_a, w_branch_b,
           rel_bias, w_out, norm_ffn_w, w_up, conv_ffn_w, conv_ffn_b, w_down, norm_final_w):
    depth = w_in.shape[0]
    d = x.shape[-1]
    o_z = CONV_A + VAL_A
    o_bg = o_z + 2 * GDN_HEADS
    o_b = o_bg + 3 * WIDTH_B
    zeros_h = jnp.zeros((GDN_HEADS,), F32)
    for l in range(depth):
        w = w_in[l].astype(BF16)
        wbg = w[:, o_z:o_bg]
        hpr = jnp.stack([jnp.concatenate([zeros_h, a_log[l].astype(F32)]),
                         jnp.concatenate([zeros_h, dt_bias[l].astype(F32)])])
        a4, qkvb, gbc, gbr = _project(
            x, norm_mix_w[l][None, :], w[:, :o_z], w[:, o_bg:o_b], wbg, wbg.T,
            conv_qkv_w[l], hpr, hpr.T)
        oa = _gated_delta(a4, gbc, gbr, gdn_norm_w[l][None, :])
        x = _merge_ffn(
            x, oa, qkvb, _rel_bias_row(rel_bias[l]), norm_mix_w[l][None, :], w[:, o_b:],
            w_branch_a[l].astype(BF16),
            w_branch_b[l].astype(BF16), w_out[l].astype(BF16), norm_ffn_w[l][None, :],
            w_up[l].astype(BF16), conv_ffn_w[l], conv_ffn_b[l][None, :], w_down[l].astype(BF16),
            norm_final_w[None, :], l == depth - 1)
    return x
```

```python
import functools

import jax
import jax.numpy as jnp
from jax import lax
from jax.experimental import pallas as pl
from jax.experimental.pallas import tpu as pltpu

CHUNK = 64
EPS = 1e-6
GDN_HEADS = 4
GDN_DK = 128
GDN_DV = 128
GDN_CONV = 4
ATT_HEADS = 8
ATT_DH = 64
ATT_BAND = 9
REL_CLIP = 128
FFN_CONV = 3

KEY_A = GDN_HEADS * GDN_DK
VAL_A = GDN_HEADS * GDN_DV
WIDTH_B = ATT_HEADS * ATT_DH
CONV_A = 2 * KEY_A + VAL_A
LEAD = (ATT_BAND - 1) * CHUNK
BAND = ATT_BAND * CHUNK

V7X_SUBLANES = 8
V7X_LANES = 128
V7X_VMEM_LIMIT_BYTES = 56 * 1024 * 1024

BIAS_ROLL = -(-(BAND + CHUNK - 1) // V7X_LANES) * V7X_LANES

PROJ_ROWS = 512
PROJ_COLS = 256
GDN_CHUNKS = 8
FFN_ROWS = 256
FFN_COLS = 256
FFN_LOOKAHEAD = 8

F32 = jnp.float32
BF16 = jnp.bfloat16
NEG_LOG2_E = -1.4426950408889634


def _dot(a, b):
    return jnp.dot(a.astype(BF16), b.astype(BF16), preferred_element_type=F32)


def _dot_nt(a, b):
    return lax.dot_general(a.astype(BF16), b.astype(BF16), (((1,), (1,)), ((), ())),
                           preferred_element_type=F32)


def _dot_tn(a, b):
    return lax.dot_general(a.astype(BF16), b.astype(BF16), (((0,), (0,)), ((), ())),
                           preferred_element_type=F32)


def _sigmoid(x):
    return 1.0 / (1.0 + jnp.exp2(x * NEG_LOG2_E))


def _softplus(x):
    return jnp.maximum(x, 0.0) + jnp.log1p(jnp.exp(-jnp.abs(x)))


def _rms_scale(x):
    return lax.rsqrt(jnp.mean(x * x, axis=-1, keepdims=True) + EPS)


def _each(fn, *lists):
    return [fn(*args) for args in zip(*lists)]


class _Interleaver:
    def __init__(self, pieces, period):
        self.pieces, self.period, self.calls = list(pieces), period, 0

    def __call__(self):
        self.calls += 1
        if self.calls % self.period == 0 and self.pieces:
            self.pieces.pop(0)()

    def flush(self):
        while self.pieces:
            self.pieces.pop(0)()


def _const_spec(shape):
    zeros = (0,) * len(shape)
    return pl.BlockSpec(shape, lambda *_: zeros, pipeline_mode=pl.Buffered(1))


def _params(n_axes):
    return pltpu.CompilerParams(dimension_semantics=("arbitrary",) * n_axes,
                                vmem_limit_bytes=V7X_VMEM_LIMIT_BYTES)


def _proj_kernel(x_ref, nw_ref, wa_ref, wb_ref, wbg_ref, wbgt_ref, cw_ref, hpr_ref, hpc_ref,
                 a4_ref, qkvb_ref, gbc_ref, gbr_ref, cbuf_ref):
    tm = x_ref.shape[1]
    halo = V7X_SUBLANES
    x = x_ref[0]
    h = (x * _rms_scale(x) * nw_ref[...]).astype(BF16)

    pc = jnp.dot(h, wbg_ref[...], preferred_element_type=F32)
    lane = lax.broadcasted_iota(jnp.int32, pc.shape, 1)
    gc = -jnp.exp(hpr_ref[0:1, :]) * _softplus(pc + hpr_ref[1:2, :])
    valc = jnp.where(lane < GDN_HEADS, _sigmoid(pc), gc)
    gbc_ref[0] = valc.reshape(tm // CHUNK, CHUNK, 2 * GDN_HEADS)

    pr = lax.dot_general(wbgt_ref[...], h, (((1,), (1,)), ((), ())),
                         preferred_element_type=F32)
    row = lax.broadcasted_iota(jnp.int32, pr.shape, 0)
    gr = -jnp.exp(hpc_ref[:, 0:1]) * _softplus(pr + hpc_ref[:, 1:2])
    valr = jnp.where(row < GDN_HEADS, _sigmoid(pr), gr)
    for c in range(tm // CHUNK):
        gbr_ref[0, c] = valr[:, c * CHUNK:(c + 1) * CHUNK]

    @pl.when(pl.program_id(1) == 0)
    def _():
        cbuf_ref[0:halo, :] = jnp.zeros((halo, CONV_A), F32)

    def project_a(blk):
        cols = slice(blk * PROJ_COLS, (blk + 1) * PROJ_COLS)
        cbuf_ref[halo:, cols] = jnp.dot(h, wa_ref[:, cols], preferred_element_type=F32)

    def project_z(blk):
        cols = slice(CONV_A + blk * PROJ_COLS, CONV_A + (blk + 1) * PROJ_COLS)
        a4_ref[0, :, cols] = jnp.dot(h, wa_ref[:, cols], preferred_element_type=F32)

    def project_b(blk):
        cols = slice(blk * PROJ_COLS, (blk + 1) * PROJ_COLS)
        pb = jnp.dot(h, wb_ref[:, cols], preferred_element_type=F32)
        if (blk + 1) * PROJ_COLS <= WIDTH_B:
            pb = pb * (ATT_DH ** -0.5)
        qkvb_ref[0, :, cols] = pb.astype(BF16)

    def conv_head(blk):
        cols = slice(blk * V7X_LANES, (blk + 1) * V7X_LANES)
        conv = cw_ref[GDN_CONV - 1:GDN_CONV, cols] * cbuf_ref[halo:halo + tm, cols]
        for tap in range(GDN_CONV - 1):
            off = halo - (GDN_CONV - 1) + tap
            conv = conv + cw_ref[tap:tap + 1, cols] * cbuf_ref[off:off + tm, cols]
        cbuf_ref[0:halo, cols] = cbuf_ref[tm:tm + halo, cols]
        act = conv * _sigmoid(conv)
        if blk < 2 * GDN_HEADS:
            inv_norm = lax.rsqrt(jnp.sum(act * act, axis=-1, keepdims=True) + EPS)
            act = act * (inv_norm * (GDN_DK ** -0.5) if blk < GDN_HEADS else inv_norm)
        a4_ref[0, :, cols] = act

    n_a = CONV_A // PROJ_COLS
    heads_per_blk = PROJ_COLS // V7X_LANES
    project_a(0)
    for blk in range(n_a):
        if blk + 1 < n_a:
            project_a(blk + 1)
        project_b(blk)
        if blk < VAL_A // PROJ_COLS:
            project_z(blk)
        for sub in range(heads_per_blk):
            conv_head(blk * heads_per_blk + sub)


def _project(x, nw, wa, wb, wbg, wbgt, cw, hpr, hpc):
    b, t, d = x.shape
    tm = PROJ_ROWS
    nc = tm // CHUNK
    return pl.pallas_call(
        _proj_kernel,
        grid=(b, t // tm),
        in_specs=[
            pl.BlockSpec((1, tm, d), lambda bi, i: (bi, i, 0)),
            _const_spec(nw.shape), _const_spec(wa.shape), _const_spec(wb.shape),
            _const_spec(wbg.shape), _const_spec(wbgt.shape), _const_spec(cw.shape),
            _const_spec(hpr.shape), _const_spec(hpc.shape),
        ],
        out_specs=[
            pl.BlockSpec((1, tm, CONV_A + VAL_A), lambda bi, i: (bi, i, 0)),
            pl.BlockSpec((1, tm, 3 * WIDTH_B), lambda bi, i: (bi, i, 0)),
            pl.BlockSpec((1, nc, CHUNK, 2 * GDN_HEADS), lambda bi, i: (bi, i, 0, 0)),
            pl.BlockSpec((1, nc, 2 * GDN_HEADS, CHUNK), lambda bi, i: (bi, i, 0, 0)),
        ],
        out_shape=[
            jax.ShapeDtypeStruct((b, t, CONV_A + VAL_A), F32),
            jax.ShapeDtypeStruct((b, t, 3 * WIDTH_B), BF16),
            jax.ShapeDtypeStruct((b, t // CHUNK, CHUNK, 2 * GDN_HEADS), F32),
            jax.ShapeDtypeStruct((b, t // CHUNK, 2 * GDN_HEADS, CHUNK), F32),
        ],
        scratch_shapes=[pltpu.VMEM((tm + V7X_SUBLANES, CONV_A), F32)],
        compiler_params=_params(2),
        name="in_proj",
    )(x, nw, wa, wb, wbg, wbgt, cw, hpr, hpc)


def _unit_lower_inverse(a, ii, jj, between):
    same16 = (ii // 16) == (jj // 16)
    same32 = (ii // 32) == (jj // 32)
    eye = jnp.where(ii == jj, 1.0, 0.0).astype(F32)
    ad = _each(lambda x: jnp.where(same16, x, 0.0), a)
    inv = _each(lambda x: eye - x, ad)
    power = _each(_dot, ad, ad)
    between()
    for level in range(3):
        inv = _each(lambda i, pi: i + pi, inv, _each(_dot, power, inv))
        if level < 2:
            power = _each(_dot, power, power)
        between()
    for off in (_each(lambda x: jnp.where(same32 & jnp.logical_not(same16), x, 0.0), a),
                _each(lambda x: jnp.where(same32, 0.0, x), a)):
        t = _each(_dot, off, inv)
        between()
        inv = _each(lambda i, t_: i - t_, inv, _each(_dot, inv, t))
        between()
    return inv


def _exact_dot_01(a, b, ones_on_left):
    x = b if ones_on_left else a
    mask = (a if ones_on_left else b).astype(BF16)
    acc = None
    for _ in range(3):
        piece = x.astype(BF16)
        x = x - piece.astype(F32)
        term = (jnp.dot(mask, piece, preferred_element_type=F32) if ones_on_left
                else jnp.dot(piece, mask, preferred_element_type=F32))
        acc = term if acc is None else acc + term
    return acc


def _gdn_kernel(a4_ref, gbc_ref, gbr_ref, gnw_ref, oa_ref,
                s_ref, wq_ref, uv_ref, pm_ref, kd_ref, sz_ref, gl_ref):
    nb = a4_ref.shape[0]
    nc = gbc_ref.shape[1]
    step = pl.program_id(0)
    slot_w = step % 2
    slot_r = 1 - slot_w
    chains = [(bi, hd) for bi in range(nb) for hd in range(GDN_HEADS)]
    n_ch = len(chains)

    @pl.when(step == 0)
    def _():
        s_ref[...] = jnp.zeros(s_ref.shape, F32)
        for ref in (wq_ref, uv_ref, pm_ref, kd_ref, sz_ref, gl_ref):
            ref[1] = jnp.zeros(ref.shape[1:], ref.dtype)

    ii = lax.broadcasted_iota(jnp.int32, (CHUNK, CHUNK), 0)
    jj = lax.broadcasted_iota(jnp.int32, (CHUNK, CHUNK), 1)
    incl = ii >= jj
    strict = ii > jj
    tri_lo = jnp.where(incl, 1.0, 0.0).astype(F32)
    tri_up = jnp.where(ii <= jj, 1.0, 0.0).astype(F32)
    gnw = gnw_ref[...]

    def recurrence(c):
        rows = pl.ds(pl.multiple_of(c * CHUNK, CHUNK), CHUNK)
        at = [c * n_ch + ch for ch in range(n_ch)]
        st = {}

        def stage_ws():
            st["s"] = [s_ref[ch] for ch in range(n_ch)]
            st["ws"] = [_dot(wq_ref[slot_r, j], s_) for j, s_ in zip(at, st["s"])]

        def stage_update():
            u = [uv_ref[slot_r, j] - ws_[:CHUNK] for j, ws_ in zip(at, st["ws"])]
            pu = [_dot(pm_ref[slot_r, j], u_) for j, u_ in zip(at, u)]
            ku = [_dot_tn(kd_ref[slot_r, j], u_) for j, u_ in zip(at, u)]
            for ch, (j, s_, ku_) in enumerate(zip(at, st["s"], ku)):
                s_ref[ch] = jnp.exp(gl_ref[slot_r, j][0:1, :]) * s_ + ku_
            for (bi, hd), j, ws_, pu_ in zip(chains, at, st["ws"], pu):
                o = ws_[CHUNK:] + pu_
                o = o * _rms_scale(o) * gnw
                oa_ref[bi, rows, hd * GDN_DV:(hd + 1) * GDN_DV] = (o * sz_ref[slot_r, j]).astype(BF16)

        return [stage_ws, stage_update]

    def pair_step(pr, carry):
        pieces = _Interleaver(recurrence(2 * pr) + recurrence(2 * pr + 1), period=2)

        cs = [2 * pr, 2 * pr + 1]
        items = [(e, bi, hd) for e in range(2) for bi, hd in chains]
        rows = [pl.ds(pl.multiple_of(c * CHUNK, CHUNK), CHUNK) for c in cs]

        gcol = [[_exact_dot_01(tri_lo, gbc_ref[bi, c], ones_on_left=True) for bi in range(nb)] for c in cs]
        grow = [[_exact_dot_01(gbr_ref[bi, c], tri_up, ones_on_left=False) for bi in range(nb)] for c in cs]

        def load(col0):
            return [a4_ref[bi, rows[e], col0 + hd * GDN_DK:col0 + (hd + 1) * GDN_DK] for e, bi, hd in items]

        q, k, v = load(0), load(KEY_A), load(2 * KEY_A)
        g_c = [gcol[e][bi][:, GDN_HEADS + hd:GDN_HEADS + hd + 1] for e, bi, hd in items]
        g_r = [grow[e][bi][GDN_HEADS + hd:GDN_HEADS + hd + 1, :] for e, bi, hd in items]
        beta = [gbc_ref[bi, cs[e]][:, hd:hd + 1] for e, bi, hd in items]
        g_last = [x[CHUNK - 1:CHUNK, :] for x in g_c]
        dec = _each(lambda gc, gr: jnp.exp(jnp.where(incl, gc - gr, 0.0)), g_c, g_r)
        gam = _each(jnp.exp, g_c)

        kq = _each(lambda k_, q_: _dot_nt(jnp.concatenate([k_, q_], axis=0), k_), k, q)
        pieces()
        a_mat = _each(lambda b_, x, d_: jnp.where(strict, b_ * x[:CHUNK] * d_, 0.0), beta, kq, dec)
        p_mat = _each(lambda x, d_: jnp.where(incl, x[CHUNK:] * d_, 0.0), kq, dec)
        inv = _unit_lower_inverse(a_mat, ii, jj, pieces)
        rhs = _each(lambda b_, g_, k_, v_: jnp.concatenate([(b_ * g_) * k_, b_ * v_], axis=1),
                    beta, gam, k, v)
        sol = _each(_dot, inv, rhs)
        pieces.flush()

        z = load(CONV_A)
        for n, (e, bi, hd) in enumerate(items):
            j = cs[e] * n_ch + bi * GDN_HEADS + hd
            wq_ref[slot_w, j] = jnp.concatenate([sol[n][:, :GDN_DK], q[n] * gam[n]], axis=0).astype(BF16)
            uv_ref[slot_w, j] = sol[n][:, GDN_DK:]
            pm_ref[slot_w, j] = p_mat[n].astype(BF16)
            kd_ref[slot_w, j] = (k[n] * jnp.exp(g_last[n] - g_c[n])).astype(BF16)
            sz_ref[slot_w, j] = z[n] * _sigmoid(z[n])
            gl_ref[slot_w, j] = jnp.broadcast_to(g_last[n], gl_ref.shape[2:])
        return carry

    lax.fori_loop(0, nc // 2, pair_step, 0)


def _gated_delta(a4, gbc, gbr, gnw):
    b, t, _ = a4.shape
    nc = GDN_CHUNKS
    rows = nc * CHUNK
    n_tiles = t // rows
    per_tile = nc * b * GDN_HEADS

    def tile_in(i):
        return jnp.minimum(i, n_tiles - 1)

    def tile_out(i):
        return jnp.maximum(i - 1, 0)

    return pl.pallas_call(
        _gdn_kernel,
        grid=(n_tiles + 1,),
        in_specs=[
            pl.BlockSpec((b, rows, CONV_A + VAL_A), lambda i: (0, tile_in(i), 0)),
            pl.BlockSpec((b, nc, CHUNK, 2 * GDN_HEADS), lambda i: (0, tile_in(i), 0, 0)),
            pl.BlockSpec((b, nc, 2 * GDN_HEADS, CHUNK), lambda i: (0, tile_in(i), 0, 0)),
            _const_spec(gnw.shape),
        ],
        out_specs=pl.BlockSpec((b, rows, VAL_A), lambda i: (0, tile_out(i), 0)),
        out_shape=jax.ShapeDtypeStruct((b, t, VAL_A), BF16),
        scratch_shapes=[
            pltpu.VMEM((b * GDN_HEADS, GDN_DK, GDN_DV), F32),
            pltpu.VMEM((2, per_tile, 2 * CHUNK, GDN_DK), BF16),
            pltpu.VMEM((2, per_tile, CHUNK, GDN_DV), F32),
            pltpu.VMEM((2, per_tile, CHUNK, CHUNK), BF16),
            pltpu.VMEM((2, per_tile, CHUNK, GDN_DK), BF16),
            pltpu.VMEM((2, per_tile, CHUNK, GDN_DV), F32),
            pltpu.VMEM((2, per_tile, V7X_SUBLANES, V7X_LANES), F32),
        ],
        compiler_params=_params(1),
        name="gated_delta",
    )(a4, gbc, gbr, gnw)


def _attention_prepare(k_ref, v_ref, rel_ref, kbuf_ref, vbuf_ref, bias_ref):
    tq = k_ref.shape[1]
    i = pl.program_id(1)

    @pl.when((i == 0) & (pl.program_id(0) == 0))
    def _():
        for hd in range(ATT_HEADS):
            rel = jnp.broadcast_to(rel_ref[hd:hd + 1, :], (CHUNK, BIAS_ROLL))
            rolled = pltpu.roll(rel, 0, 1, stride=1, stride_axis=0)
            bias_ref[hd // 2, (hd % 2) * CHUNK:(hd % 2 + 1) * CHUNK, :] = rolled[:, :BAND]

    @pl.when(i == 0)
    def _():
        kbuf_ref[0:LEAD, :] = jnp.zeros((LEAD, WIDTH_B), BF16)
        vbuf_ref[0:LEAD, :] = jnp.zeros((LEAD, WIDTH_B), BF16)

    @pl.when(i > 0)
    def _():
        kbuf_ref[0:LEAD, :] = kbuf_ref[tq:tq + LEAD, :]
        vbuf_ref[0:LEAD, :] = vbuf_ref[tq:tq + LEAD, :]

    kbuf_ref[LEAD:, :] = k_ref[0]
    vbuf_ref[LEAD:, :] = v_ref[0]


def _attention_chunk(c, q_ref, kbuf_ref, vbuf_ref, bias_ref, ob_ref):
    tq = q_ref.shape[1]
    i = pl.program_id(1)
    rows = slice(c * CHUNK, (c + 1) * CHUNK)
    band = slice(c * CHUNK, c * CHUNK + BAND)
    pairs = list(range(ATT_HEADS // 2))
    lanes = [slice(pr * 2 * ATT_DH, (pr + 1) * 2 * ATT_DH) for pr in pairs]
    first = lax.broadcasted_iota(jnp.int32, (CHUNK, 2 * ATT_DH), 1) < ATT_DH
    st = {}

    def scores():
        key = lax.broadcasted_iota(jnp.int32, (1, BAND), 1)
        valid = (i * tq - LEAD + c * CHUNK + key) >= 0
        zero_q = jnp.zeros((CHUNK, 2 * ATT_DH), BF16)

        def stacked_q(ln):
            q = q_ref[0, rows, ln]
            return jnp.concatenate([jnp.where(first, q, zero_q), jnp.where(first, zero_q, q)], axis=0)

        s = [_dot_nt(stacked_q(ln), kbuf_ref[band, ln]) for ln in lanes]
        s = [jnp.where(valid, x + bias_ref[pr], -jnp.inf) for x, pr in zip(s, pairs)]
        p = [jnp.exp(x - jnp.max(x, axis=-1, keepdims=True)) for x in s]
        st["denom"] = [jnp.sum(x, axis=-1, keepdims=True) for x in p]
        st["p"] = [x.astype(BF16) for x in p]

    def values():
        pv = [_dot(x, vbuf_ref[band, ln]) for x, ln in zip(st["p"], lanes)]
        for x, dn, ln in zip(pv, st["denom"], lanes):
            x = x / dn
            ob_ref[rows, ln] = jnp.where(first, x[:CHUNK], x[CHUNK:]).astype(BF16)

    return [scores, values]


def _ffn_kernel(final_norm, x_ref, oa_ref, q_ref, k_ref, v_ref, rel_ref, nmw_ref, wg_ref, wa_ref,
                wb_ref, wo_ref, nfw_ref, wup_ref, cfw_ref, cfb_ref, wdn_ref, nlw_ref, out_ref,
                ubuf_ref, kbuf_ref, vbuf_ref, bias_ref, ob_ref):
    tm = x_ref.shape[1]
    d = x_ref.shape[2]
    dff = wdn_ref.shape[0]
    halo = V7X_SUBLANES
    x = x_ref[0]

    _attention_prepare(k_ref, v_ref, rel_ref, kbuf_ref, vbuf_ref, bias_ref)
    attention = _Interleaver(
        [piece for c in range(tm // CHUNK)
         for piece in _attention_chunk(c, q_ref, kbuf_ref, vbuf_ref, bias_ref, ob_ref)], period=1)

    h = (x * _rms_scale(x) * nmw_ref[...]).astype(BF16)
    gates = []
    for blk in range(2 * d // FFN_COLS):
        gates.append(jnp.dot(h, wg_ref[:, blk * FFN_COLS:(blk + 1) * FFN_COLS], preferred_element_type=F32))
        attention()
    ya = jnp.dot(oa_ref[0], wa_ref[...], preferred_element_type=F32)
    attention.flush()
    gates = jnp.concatenate(gates, axis=1)
    yb = jnp.dot(ob_ref[...], wb_ref[...], preferred_element_type=F32)
    mix = _sigmoid(gates[:, :d]) * ya + _sigmoid(gates[:, d:]) * yb
    x1 = x + _dot(mix, wo_ref[...])

    @pl.when(pl.program_id(1) == 0)
    def _():
        ubuf_ref[0:halo, :] = jnp.zeros((halo, 2 * dff), F32)

    h2 = (x1 * _rms_scale(x1) * nfw_ref[...]).astype(BF16)

    def gate_up_cols(blk):
        return [slice(half * dff + blk * FFN_COLS, half * dff + (blk + 1) * FFN_COLS) for half in range(2)]

    def project_up(blk):
        for cols in gate_up_cols(blk):
            ubuf_ref[halo:, cols] = jnp.dot(h2, wup_ref[:, cols], preferred_element_type=F32)

    def conv_block(cols):
        acc = cfb_ref[:, cols] + cfw_ref[FFN_CONV - 1:FFN_CONV, cols] * ubuf_ref[halo:halo + tm, cols]
        for tap in range(FFN_CONV - 1):
            off = halo - (FFN_CONV - 1) + tap
            acc = acc + cfw_ref[tap:tap + 1, cols] * ubuf_ref[off:off + tm, cols]
        ubuf_ref[0:halo, cols] = ubuf_ref[tm:tm + halo, cols]
        return acc

    n_blk = dff // FFN_COLS
    for blk in range(min(FFN_LOOKAHEAD, n_blk)):
        project_up(blk)
    x2 = x1
    for blk in range(n_blk):
        if blk + FFN_LOOKAHEAD < n_blk:
            project_up(blk + FFN_LOOKAHEAD)
        gate, up = [conv_block(cols) for cols in gate_up_cols(blk)]
        act = (gate * _sigmoid(gate) * up).astype(BF16)
        x2 = x2 + jnp.dot(act, wdn_ref[blk * FFN_COLS:(blk + 1) * FFN_COLS, :], preferred_element_type=F32)
    if final_norm:
        x2 = x2 * _rms_scale(x2) * nlw_ref[...]
    out_ref[0] = x2


def _merge_ffn(x, oa, qkvb, rel, nmw, wg, wa, wb, wo, nfw, wup, cfw, cfb, wdn, nlw, final_norm):
    b, t, d = x.shape
    tm = FFN_ROWS
    dff = wdn.shape[0]
    consts = (rel, nmw, wg, wa, wb, wo, nfw, wup, cfw, cfb, wdn, nlw)
    return pl.pallas_call(
        functools.partial(_ffn_kernel, final_norm),
        grid=(b, t // tm),
        in_specs=[
            pl.BlockSpec((1, tm, d), lambda bi, i: (bi, i, 0)),
            pl.BlockSpec((1, tm, VAL_A), lambda bi, i: (bi, i, 0)),
            pl.BlockSpec((1, tm, WIDTH_B), lambda bi, i: (bi, i, 0)),
            pl.BlockSpec((1, tm, WIDTH_B), lambda bi, i: (bi, i, 1)),
            pl.BlockSpec((1, tm, WIDTH_B), lambda bi, i: (bi, i, 2)),
        ] + [_const_spec(c.shape) for c in consts],
        out_specs=pl.BlockSpec((1, tm, d), lambda bi, i: (bi, i, 0)),
        out_shape=jax.ShapeDtypeStruct((b, t, d), F32),
        scratch_shapes=[
            pltpu.VMEM((tm + V7X_SUBLANES, 2 * dff), F32),
            pltpu.VMEM((LEAD + tm, WIDTH_B), BF16),
            pltpu.VMEM((LEAD + tm, WIDTH_B), BF16),
            pltpu.VMEM((ATT_HEADS // 2, 2 * CHUNK, BAND), F32),
            pltpu.VMEM((tm, WIDTH_B), BF16),
        ],
        compiler_params=_params(2),
        name="merge_ffn",
    )(x, oa, qkvb, qkvb, qkvb, *consts)


def _rel_bias_row(rel_table):
    n = jnp.arange(BIAS_ROLL)
    dist = jnp.where(n < BAND, jnp.clip(LEAD - n, -REL_CLIP, REL_CLIP), REL_CLIP)
    return rel_table.astype(F32)[:, dist + REL_CLIP]


def kernel(x, norm_mix_w, w_in, conv_qkv_w, a_log, dt_bias, gdn_norm_w, w_branch_a, w_branch_b,
           rel_bias, w_out, norm_ffn_w, w_up, conv_ffn_w, conv_ffn_b, w_down, norm_final_w):
    depth = w_in.shape[0]
    d = x.shape[-1]
    o_z = CONV_A + VAL_A
    o_bg = o_z + 2 * GDN_HEADS
    o_b = o_bg + 3 * WIDTH_B
    zeros_h = jnp.zeros((GDN_HEADS,), F32)
    for l in range(depth):
        w_l = w_in[l]

        def w_cols(lo, hi, w_l=w_l):
            return w_l[:, lo:hi].astype(BF16)

        wbg = w_cols(o_z, o_bg)
        hpr = jnp.stack([jnp.concatenate([zeros_h, a_log[l].astype(F32)]),
                         jnp.concatenate([zeros_h, dt_bias[l].astype(F32)])])
        a4, qkvb, gbc, gbr = _project(
            x, norm_mix_w[l][None, :], w_cols(0, o_z), w_cols(o_bg, o_b), wbg, wbg.T,
            conv_qkv_w[l], hpr, hpr.T)
        oa = _gated_delta(a4, gbc, gbr, gdn_norm_w[l][None, :])
        x = _merge_ffn(
            x, oa, qkvb, _rel_bias_row(rel_bias[l]), norm_mix_w[l][None, :], w_cols(o_b, w_l.shape[1]),
            w_branch_a[l].astype(BF16),
            w_branch_b[l].astype(BF16), w_out[l].astype(BF16), norm_ffn_w[l][None, :],
            w_up[l].astype(BF16), conv_ffn_w[l], conv_ffn_b[l][None, :], w_down[l].astype(BF16),
            norm_final_w[None, :], l == depth - 1)
    return x
```

```python
import functools

import jax
import jax.numpy as jnp
from jax import lax
from jax.experimental import pallas as pl
from jax.experimental.pallas import tpu as pltpu

CHUNK = 64
EPS = 1e-6
GDN_HEADS = 4
GDN_DK = 128
GDN_DV = 128
GDN_CONV = 4
ATT_HEADS = 8
ATT_DH = 64
ATT_BAND = 9
REL_CLIP = 128
FFN_CONV = 3

KEY_A = GDN_HEADS * GDN_DK
VAL_A = GDN_HEADS * GDN_DV
WIDTH_B = ATT_HEADS * ATT_DH
CONV_A = 2 * KEY_A + VAL_A
LEAD = (ATT_BAND - 1) * CHUNK
BAND = ATT_BAND * CHUNK

V7X_SUBLANES = 8
V7X_LANES = 128
V7X_VMEM_LIMIT_BYTES = 56 * 1024 * 1024

BIAS_ROLL = -(-(BAND + CHUNK - 1) // V7X_LANES) * V7X_LANES

PROJ_ROWS = 512
PROJ_COLS = 256
GDN_CHUNKS = 8
GDN_GROUP = 4
FFN_ROWS = 256
FFN_COLS = 256
FFN_LOOKAHEAD = 8

F32 = jnp.float32
BF16 = jnp.bfloat16
NEG_LOG2_E = -1.4426950408889634


def _dot(a, b):
    return jnp.dot(a.astype(BF16), b.astype(BF16), preferred_element_type=F32)


def _dot_nt(a, b):
    return lax.dot_general(a.astype(BF16), b.astype(BF16), (((1,), (1,)), ((), ())),
                           preferred_element_type=F32)


def _dot_tn(a, b):
    return lax.dot_general(a.astype(BF16), b.astype(BF16), (((0,), (0,)), ((), ())),
                           preferred_element_type=F32)


def _sigmoid(x):
    return 1.0 / (1.0 + jnp.exp2(x * NEG_LOG2_E))


def _softplus(x):
    return jnp.maximum(x, 0.0) + jnp.log1p(jnp.exp(-jnp.abs(x)))


def _rms_scale(x):
    return lax.rsqrt(jnp.mean(x * x, axis=-1, keepdims=True) + EPS)


def _each(fn, *lists):
    return [fn(*args) for args in zip(*lists)]


class _Interleaver:
    def __init__(self, pieces, period, burst=1):
        self.pieces, self.period, self.burst, self.calls = list(pieces), period, burst, 0

    def __call__(self):
        self.calls += 1
        if self.calls % self.period == 0:
            for _ in range(min(self.burst, len(self.pieces))):
                self.pieces.pop(0)()

    def flush(self):
        while self.pieces:
            self.pieces.pop(0)()


def _const_spec(shape):
    zeros = (0,) * len(shape)
    return pl.BlockSpec(shape, lambda *_: zeros, pipeline_mode=pl.Buffered(1))


def _params(n_axes):
    return pltpu.CompilerParams(dimension_semantics=("arbitrary",) * n_axes,
                                vmem_limit_bytes=V7X_VMEM_LIMIT_BYTES)


def _proj_kernel(x_ref, nw_ref, wa_ref, wb_ref, wbg_ref, wbgt_ref, cw_ref, hpr_ref, hpc_ref,
                 a4_ref, qkvb_ref, gbc_ref, gbr_ref, cbuf_ref):
    tm = x_ref.shape[1]
    halo = V7X_SUBLANES
    x = x_ref[0]
    h = (x * _rms_scale(x) * nw_ref[...]).astype(BF16)

    pc = jnp.dot(h, wbg_ref[...], preferred_element_type=F32)
    lane = lax.broadcasted_iota(jnp.int32, pc.shape, 1)
    gc = -jnp.exp(hpr_ref[0:1, :]) * _softplus(pc + hpr_ref[1:2, :])
    valc = jnp.where(lane < GDN_HEADS, _sigmoid(pc), gc)
    gbc_ref[0] = valc.reshape(tm // CHUNK, CHUNK, 2 * GDN_HEADS)

    pr = lax.dot_general(wbgt_ref[...], h, (((1,), (1,)), ((), ())),
                         preferred_element_type=F32)
    row = lax.broadcasted_iota(jnp.int32, pr.shape, 0)
    gr = -jnp.exp(hpc_ref[:, 0:1]) * _softplus(pr + hpc_ref[:, 1:2])
    valr = jnp.where(row < GDN_HEADS, _sigmoid(pr), gr)
    for c in range(tm // CHUNK):
        gbr_ref[0, c] = valr[:, c * CHUNK:(c + 1) * CHUNK]

    cbuf_ref[0:halo, :] = jnp.where(pl.program_id(1) == 0, 0.0, cbuf_ref[0:halo, :])

    def project_a(blk):
        cols = slice(blk * PROJ_COLS, (blk + 1) * PROJ_COLS)
        cbuf_ref[halo:, cols] = jnp.dot(h, wa_ref[:, cols], preferred_element_type=F32)

    def project_z(blk):
        cols = slice(CONV_A + blk * PROJ_COLS, CONV_A + (blk + 1) * PROJ_COLS)
        a4_ref[0, :, cols] = jnp.dot(h, wa_ref[:, cols], preferred_element_type=F32)

    def project_b(blk):
        cols = slice(blk * PROJ_COLS, (blk + 1) * PROJ_COLS)
        pb = jnp.dot(h, wb_ref[:, cols], preferred_element_type=F32)
        if (blk + 1) * PROJ_COLS <= WIDTH_B:
            pb = pb * (ATT_DH ** -0.5)
        qkvb_ref[0, :, cols] = pb.astype(BF16)

    def conv_head(blk):
        cols = slice(blk * V7X_LANES, (blk + 1) * V7X_LANES)
        conv = cw_ref[GDN_CONV - 1:GDN_CONV, cols] * cbuf_ref[halo:halo + tm, cols]
        for tap in range(GDN_CONV - 1):
            off = halo - (GDN_CONV - 1) + tap
            conv = conv + cw_ref[tap:tap + 1, cols] * cbuf_ref[off:off + tm, cols]
        cbuf_ref[0:halo, cols] = cbuf_ref[tm:tm + halo, cols]
        act = conv * _sigmoid(conv)
        if blk < 2 * GDN_HEADS:
            inv_norm = lax.rsqrt(jnp.sum(act * act, axis=-1, keepdims=True) + EPS)
            act = act * (inv_norm * (GDN_DK ** -0.5) if blk < GDN_HEADS else inv_norm)
        a4_ref[0, :, cols] = act

    n_a = CONV_A // PROJ_COLS
    heads_per_blk = PROJ_COLS // V7X_LANES
    project_a(0)
    for blk in range(n_a):
        if blk + 1 < n_a:
            project_a(blk + 1)
        project_b(blk)
        if blk < VAL_A // PROJ_COLS:
            project_z(blk)
        for sub in range(heads_per_blk):
            conv_head(blk * heads_per_blk + sub)


def _project(x, nw, wa, wb, wbg, wbgt, cw, hpr, hpc):
    b, t, d = x.shape
    tm = PROJ_ROWS
    nc = tm // CHUNK
    return pl.pallas_call(
        _proj_kernel,
        grid=(b, t // tm),
        in_specs=[
            pl.BlockSpec((1, tm, d), lambda bi, i: (bi, i, 0)),
            _const_spec(nw.shape), _const_spec(wa.shape), _const_spec(wb.shape),
            _const_spec(wbg.shape), _const_spec(wbgt.shape), _const_spec(cw.shape),
            _const_spec(hpr.shape), _const_spec(hpc.shape),
        ],
        out_specs=[
            pl.BlockSpec((1, tm, CONV_A + VAL_A), lambda bi, i: (bi, i, 0)),
            pl.BlockSpec((1, tm, 3 * WIDTH_B), lambda bi, i: (bi, i, 0)),
            pl.BlockSpec((1, nc, CHUNK, 2 * GDN_HEADS), lambda bi, i: (bi, i, 0, 0)),
            pl.BlockSpec((1, nc, 2 * GDN_HEADS, CHUNK), lambda bi, i: (bi, i, 0, 0)),
        ],
        out_shape=[
            jax.ShapeDtypeStruct((b, t, CONV_A + VAL_A), F32),
            jax.ShapeDtypeStruct((b, t, 3 * WIDTH_B), BF16),
            jax.ShapeDtypeStruct((b, t // CHUNK, CHUNK, 2 * GDN_HEADS), F32),
            jax.ShapeDtypeStruct((b, t // CHUNK, 2 * GDN_HEADS, CHUNK), F32),
        ],
        scratch_shapes=[pltpu.VMEM((tm + V7X_SUBLANES, CONV_A), F32)],
        compiler_params=_params(2),
        name="in_proj",
    )(x, nw, wa, wb, wbg, wbgt, cw, hpr, hpc)


def _unit_lower_inverse(a, ii, jj, between):
    same16 = (ii // 16) == (jj // 16)
    same32 = (ii // 32) == (jj // 32)
    eye = jnp.where(ii == jj, 1.0, 0.0).astype(F32)
    ad = _each(lambda x: jnp.where(same16, x, 0.0), a)
    inv = _each(lambda x: eye - x, ad)
    power = _each(_dot, ad, ad)
    between()
    for level in range(3):
        inv = _each(lambda i, pi: i + pi, inv, _each(_dot, power, inv))
        if level < 2:
            power = _each(_dot, power, power)
        between()
    for off in (_each(lambda x: jnp.where(same32 & jnp.logical_not(same16), x, 0.0), a),
                _each(lambda x: jnp.where(same32, 0.0, x), a)):
        t = _each(_dot, off, inv)
        between()
        inv = _each(lambda i, t_: i - t_, inv, _each(_dot, inv, t))
        between()
    return inv


def _exact_dot_01(a, b, ones_on_left):
    x = b if ones_on_left else a
    mask = (a if ones_on_left else b).astype(BF16)
    acc = None
    for _ in range(3):
        piece = x.astype(BF16)
        x = x - piece.astype(F32)
        term = (jnp.dot(mask, piece, preferred_element_type=F32) if ones_on_left
                else jnp.dot(piece, mask, preferred_element_type=F32))
        acc = term if acc is None else acc + term
    return acc


def _gdn_kernel(a4_ref, gbc_ref, gbr_ref, gnw_ref, oa_ref,
                s_ref, wq_ref, uv_ref, pm_ref, kd_ref, sz_ref, gl_ref):
    nb = a4_ref.shape[0]
    nc = gbc_ref.shape[1]
    step = pl.program_id(0)
    slot_w = step % 2
    slot_r = 1 - slot_w
    chains = [(bi, hd) for bi in range(nb) for hd in range(GDN_HEADS)]
    n_ch = len(chains)

    @pl.when(step == 0)
    def _():
        s_ref[...] = jnp.zeros(s_ref.shape, F32)
        for ref in (wq_ref, uv_ref, pm_ref, kd_ref, sz_ref, gl_ref):
            ref[1] = jnp.zeros(ref.shape[1:], ref.dtype)

    ii = lax.broadcasted_iota(jnp.int32, (CHUNK, CHUNK), 0)
    jj = lax.broadcasted_iota(jnp.int32, (CHUNK, CHUNK), 1)
    incl = ii >= jj
    strict = ii > jj
    tri_lo = jnp.where(incl, 1.0, 0.0).astype(F32)
    tri_up = jnp.where(ii <= jj, 1.0, 0.0).astype(F32)
    gnw = gnw_ref[...]

    def recurrence(c):
        rows = pl.ds(pl.multiple_of(c * CHUNK, CHUNK), CHUNK)
        at = [c * n_ch + ch for ch in range(n_ch)]
        st = {}

        def stage_ws():
            st["s"] = [s_ref[ch] for ch in range(n_ch)]
            st["ws"] = [_dot(wq_ref[slot_r, j], s_) for j, s_ in zip(at, st["s"])]

        def stage_update():
            u = [uv_ref[slot_r, j] - ws_[:CHUNK] for j, ws_ in zip(at, st["ws"])]
            pu = [_dot(pm_ref[slot_r, j], u_) for j, u_ in zip(at, u)]
            ku = [_dot_tn(kd_ref[slot_r, j], u_) for j, u_ in zip(at, u)]
            for ch, (j, s_, ku_) in enumerate(zip(at, st["s"], ku)):
                s_ref[ch] = jnp.exp(gl_ref[slot_r, j][0:1, :]) * s_ + ku_
            for (bi, hd), j, ws_, pu_ in zip(chains, at, st["ws"], pu):
                o = ws_[CHUNK:] + pu_
                o = o * _rms_scale(o) * gnw
                oa_ref[bi, rows, hd * GDN_DV:(hd + 1) * GDN_DV] = (o * sz_ref[slot_r, j]).astype(BF16)

        return [stage_ws, stage_update]

    def pair_step(pr, carry):
        cs = [GDN_GROUP * pr + e for e in range(GDN_GROUP)]

        pieces = _Interleaver([piece for c in cs for piece in recurrence(c)],
                              period=max(1, 4 // GDN_GROUP), burst=max(1, GDN_GROUP // 4))

        items = [(e, bi, hd) for e in range(GDN_GROUP) for bi, hd in chains]
        rows = [pl.ds(pl.multiple_of(c * CHUNK, CHUNK), CHUNK) for c in cs]

        gcol = [[_exact_dot_01(tri_lo, gbc_ref[bi, c], ones_on_left=True) for bi in range(nb)] for c in cs]
        grow = [[_exact_dot_01(gbr_ref[bi, c], tri_up, ones_on_left=False) for bi in range(nb)] for c in cs]

        def load(col0):
            return [a4_ref[bi, rows[e], col0 + hd * GDN_DK:col0 + (hd + 1) * GDN_DK] for e, bi, hd in items]

        q, k, v = load(0), load(KEY_A), load(2 * KEY_A)
        g_c = [gcol[e][bi][:, GDN_HEADS + hd:GDN_HEADS + hd + 1] for e, bi, hd in items]
        g_r = [grow[e][bi][GDN_HEADS + hd:GDN_HEADS + hd + 1, :] for e, bi, hd in items]
        beta = [gbc_ref[bi, cs[e]][:, hd:hd + 1] for e, bi, hd in items]
        g_last = [x[CHUNK - 1:CHUNK, :] for x in g_c]
        dec = _each(lambda gc, gr: jnp.exp(jnp.where(incl, gc - gr, 0.0)), g_c, g_r)
        gam = _each(jnp.exp, g_c)

        kq = _each(lambda k_, q_: _dot_nt(jnp.concatenate([k_, q_], axis=0), k_), k, q)
        pieces()
        a_mat = _each(lambda b_, x, d_: jnp.where(strict, b_ * x[:CHUNK] * d_, 0.0), beta, kq, dec)
        p_mat = _each(lambda x, d_: jnp.where(incl, x[CHUNK:] * d_, 0.0), kq, dec)
        inv = _unit_lower_inverse(a_mat, ii, jj, pieces)
        rhs = _each(lambda b_, g_, k_, v_: jnp.concatenate([(b_ * g_) * k_, b_ * v_], axis=1),
                    beta, gam, k, v)
        sol = _each(_dot, inv, rhs)
        pieces.flush()

        z = load(CONV_A)
        for n, (e, bi, hd) in enumerate(items):
            j = cs[e] * n_ch + bi * GDN_HEADS + hd
            wq_ref[slot_w, j] = jnp.concatenate([sol[n][:, :GDN_DK], q[n] * gam[n]], axis=0).astype(BF16)
            uv_ref[slot_w, j] = sol[n][:, GDN_DK:]
            pm_ref[slot_w, j] = p_mat[n].astype(BF16)
            kd_ref[slot_w, j] = (k[n] * jnp.exp(g_last[n] - g_c[n])).astype(BF16)
            sz_ref[slot_w, j] = z[n] * _sigmoid(z[n])
            gl_ref[slot_w, j] = jnp.broadcast_to(g_last[n], gl_ref.shape[2:])
        return carry

    lax.fori_loop(0, nc // GDN_GROUP, pair_step, 0)


def _gated_delta(a4, gbc, gbr, gnw):
    b, t, _ = a4.shape
    nc = GDN_CHUNKS
    rows = nc * CHUNK
    n_tiles = t // rows
    per_tile = nc * b * GDN_HEADS

    def tile_in(i):
        return jnp.minimum(i, n_tiles - 1)

    def tile_out(i):
        return jnp.maximum(i - 1, 0)

    return pl.pallas_call(
        _gdn_kernel,
        grid=(n_tiles + 1,),
        in_specs=[
            pl.BlockSpec((b, rows, CONV_A + VAL_A), lambda i: (0, tile_in(i), 0)),
            pl.BlockSpec((b, nc, CHUNK, 2 * GDN_HEADS), lambda i: (0, tile_in(i), 0, 0)),
            pl.BlockSpec((b, nc, 2 * GDN_HEADS, CHUNK), lambda i: (0, tile_in(i), 0, 0)),
            _const_spec(gnw.shape),
        ],
        out_specs=pl.BlockSpec((b, rows, VAL_A), lambda i: (0, tile_out(i), 0)),
        out_shape=jax.ShapeDtypeStruct((b, t, VAL_A), BF16),
        scratch_shapes=[
            pltpu.VMEM((b * GDN_HEADS, GDN_DK, GDN_DV), F32),
            pltpu.VMEM((2, per_tile, 2 * CHUNK, GDN_DK), BF16),
            pltpu.VMEM((2, per_tile, CHUNK, GDN_DV), F32),
            pltpu.VMEM((2, per_tile, CHUNK, CHUNK), BF16),
            pltpu.VMEM((2, per_tile, CHUNK, GDN_DK), BF16),
            pltpu.VMEM((2, per_tile, CHUNK, GDN_DV), F32),
            pltpu.VMEM((2, per_tile, V7X_SUBLANES, V7X_LANES), F32),
        ],
        compiler_params=_params(1),
        name="gated_delta",
    )(a4, gbc, gbr, gnw)


def _attention_prepare(k_ref, v_ref, rel_ref, kbuf_ref, vbuf_ref, bias_ref):
    tq = k_ref.shape[1]
    i = pl.program_id(1)

    @pl.when((i == 0) & (pl.program_id(0) == 0))
    def _():
        for hd in range(ATT_HEADS):
            rel = jnp.broadcast_to(rel_ref[hd:hd + 1, :], (CHUNK, BIAS_ROLL))
            rolled = pltpu.roll(rel, 0, 1, stride=1, stride_axis=0)
            bias_ref[hd // 2, (hd % 2) * CHUNK:(hd % 2 + 1) * CHUNK, :] = rolled[:, :BAND]

    zeros = jnp.zeros((LEAD, WIDTH_B), BF16)
    kbuf_ref[0:LEAD, :] = jnp.where(i == 0, zeros, kbuf_ref[tq:tq + LEAD, :])
    vbuf_ref[0:LEAD, :] = jnp.where(i == 0, zeros, vbuf_ref[tq:tq + LEAD, :])
    kbuf_ref[LEAD:, :] = k_ref[0]
    vbuf_ref[LEAD:, :] = v_ref[0]


def _attention_chunk(c, q_ref, kbuf_ref, vbuf_ref, bias_ref, ob_ref):
    tq = q_ref.shape[1]
    i = pl.program_id(1)
    rows = slice(c * CHUNK, (c + 1) * CHUNK)
    band = slice(c * CHUNK, c * CHUNK + BAND)
    pairs = list(range(ATT_HEADS // 2))
    lanes = [slice(pr * 2 * ATT_DH, (pr + 1) * 2 * ATT_DH) for pr in pairs]
    first = lax.broadcasted_iota(jnp.int32, (CHUNK, 2 * ATT_DH), 1) < ATT_DH
    st = {}

    def scores():
        key = lax.broadcasted_iota(jnp.int32, (1, BAND), 1)
        valid = (i * tq - LEAD + c * CHUNK + key) >= 0
        zero_q = jnp.zeros((CHUNK, 2 * ATT_DH), BF16)

        def stacked_q(ln):
            q = q_ref[0, rows, ln]
            return jnp.concatenate([jnp.where(first, q, zero_q), jnp.where(first, zero_q, q)], axis=0)

        s = [_dot_nt(stacked_q(ln), kbuf_ref[band, ln]) for ln in lanes]
        s = [jnp.where(valid, x + bias_ref[pr], -jnp.inf) for x, pr in zip(s, pairs)]
        p = [jnp.exp(x - jnp.max(x, axis=-1, keepdims=True)) for x in s]
        st["denom"] = [jnp.sum(x, axis=-1, keepdims=True) for x in p]
        st["p"] = [x.astype(BF16) for x in p]

    def values():
        pv = [_dot(x, vbuf_ref[band, ln]) for x, ln in zip(st["p"], lanes)]
        for x, dn, ln in zip(pv, st["denom"], lanes):
            x = x / dn
            ob_ref[rows, ln] = jnp.where(first, x[:CHUNK], x[CHUNK:]).astype(BF16)

    return [scores, values]


def _ffn_kernel(final_norm, x_ref, oa_ref, q_ref, k_ref, v_ref, rel_ref, nmw_ref, wg_ref, wa_ref,
                wb_ref, wo_ref, nfw_ref, wup_ref, cfw_ref, cfb_ref, wdn_ref, nlw_ref, out_ref,
                ubuf_ref, kbuf_ref, vbuf_ref, bias_ref, ob_ref):
    tm = x_ref.shape[1]
    d = x_ref.shape[2]
    dff = wdn_ref.shape[0]
    halo = V7X_SUBLANES
    x = x_ref[0]

    _attention_prepare(k_ref, v_ref, rel_ref, kbuf_ref, vbuf_ref, bias_ref)
    ubuf_ref[0:halo, :] = jnp.where(pl.program_id(1) == 0, 0.0, ubuf_ref[0:halo, :])
    attention = _Interleaver(
        [piece for c in range(tm // CHUNK)
         for piece in _attention_chunk(c, q_ref, kbuf_ref, vbuf_ref, bias_ref, ob_ref)], period=1)

    h = (x * _rms_scale(x) * nmw_ref[...]).astype(BF16)
    gates = []
    for blk in range(2 * d // FFN_COLS):
        gates.append(jnp.dot(h, wg_ref[:, blk * FFN_COLS:(blk + 1) * FFN_COLS], preferred_element_type=F32))
        attention()
    ya = jnp.dot(oa_ref[0], wa_ref[...], preferred_element_type=F32)
    attention.flush()
    gates = jnp.concatenate(gates, axis=1)
    yb = jnp.dot(ob_ref[...], wb_ref[...], preferred_element_type=F32)
    mix = _sigmoid(gates[:, :d]) * ya + _sigmoid(gates[:, d:]) * yb
    x1 = x + _dot(mix, wo_ref[...])

    h2 = (x1 * _rms_scale(x1) * nfw_ref[...]).astype(BF16)

    def gate_up_cols(blk):
        return [slice(half * dff + blk * FFN_COLS, half * dff + (blk + 1) * FFN_COLS) for half in range(2)]

    def project_up(blk):
        for cols in gate_up_cols(blk):
            ubuf_ref[halo:, cols] = jnp.dot(h2, wup_ref[:, cols], preferred_element_type=F32)

    def conv_block(cols):
        acc = cfb_ref[:, cols] + cfw_ref[FFN_CONV - 1:FFN_CONV, cols] * ubuf_ref[halo:halo + tm, cols]
        for tap in range(FFN_CONV - 1):
            off = halo - (FFN_CONV - 1) + tap
            acc = acc + cfw_ref[tap:tap + 1, cols] * ubuf_ref[off:off + tm, cols]
        ubuf_ref[0:halo, cols] = ubuf_ref[tm:tm + halo, cols]
        return acc

    n_blk = dff // FFN_COLS
    for blk in range(min(FFN_LOOKAHEAD, n_blk)):
        project_up(blk)
    x2 = x1
    for blk in range(n_blk):
        if blk + FFN_LOOKAHEAD < n_blk:
            project_up(blk + FFN_LOOKAHEAD)
        gate, up = [conv_block(cols) for cols in gate_up_cols(blk)]
        act = (gate * _sigmoid(gate) * up).astype(BF16)
        x2 = x2 + jnp.dot(act, wdn_ref[blk * FFN_COLS:(blk + 1) * FFN_COLS, :], preferred_element_type=F32)
    if final_norm:
        x2 = x2 * _rms_scale(x2) * nlw_ref[...]
    out_ref[0] = x2


def _merge_ffn(x, oa, qkvb, rel, nmw, wg, wa, wb, wo, nfw, wup, cfw, cfb, wdn, nlw, final_norm):
    b, t, d = x.shape
    tm = FFN_ROWS
    dff = wdn.shape[0]
    consts = (rel, nmw, wg, wa, wb, wo, nfw, wup, cfw, cfb, wdn, nlw)
    return pl.pallas_call(
        functools.partial(_ffn_kernel, final_norm),
        grid=(b, t // tm),
        in_specs=[
            pl.BlockSpec((1, tm, d), lambda bi, i: (bi, i, 0)),
            pl.BlockSpec((1, tm, VAL_A), lambda bi, i: (bi, i, 0)),
            pl.BlockSpec((1, tm, WIDTH_B), lambda bi, i: (bi, i, 0)),
            pl.BlockSpec((1, tm, WIDTH_B), lambda bi, i: (bi, i, 1)),
            pl.BlockSpec((1, tm, WIDTH_B), lambda bi, i: (bi, i, 2)),
        ] + [_const_spec(c.shape) for c in consts],
        out_specs=pl.BlockSpec((1, tm, d), lambda bi, i: (bi, i, 0)),
        out_shape=jax.ShapeDtypeStruct((b, t, d), F32),
        scratch_shapes=[
            pltpu.VMEM((tm + V7X_SUBLANES, 2 * dff), F32),
            pltpu.VMEM((LEAD + tm, WIDTH_B), BF16),
            pltpu.VMEM((LEAD + tm, WIDTH_B), BF16),
            pltpu.VMEM((ATT_HEADS // 2, 2 * CHUNK, BAND), F32),
            pltpu.VMEM((tm, WIDTH_B), BF16),
        ],
        compiler_params=_params(2),
        name="merge_ffn",
    )(x, oa, qkvb, qkvb, qkvb, *consts)


def _rel_bias_row(rel_table):
    n = jnp.arange(BIAS_ROLL)
    dist = jnp.where(n < BAND, jnp.clip(LEAD - n, -REL_CLIP, REL_CLIP), REL_CLIP)
    return rel_table.astype(F32)[:, dist + REL_CLIP]


def kernel(x, norm_mix_w, w_in, conv_qkv_w, a_log, dt_bias, gdn_norm_w, w_branch_a, w_branch_b,
           rel_bias, w_out, norm_ffn_w, w_up, conv_ffn_w, conv_ffn_b, w_down, norm_final_w):
    depth = w_in.shape[0]
    d = x.shape[-1]
    o_z = CONV_A + VAL_A
    o_bg = o_z + 2 * GDN_HEADS
    o_b = o_bg + 3 * WIDTH_B
    zeros_h = jnp.zeros((GDN_HEADS,), F32)
    for l in range(depth):
        w_l = w_in[l]

        def w_cols(lo, hi, w_l=w_l):
            return w_l[:, lo:hi].astype(BF16)

        wbg = w_cols(o_z, o_bg)
        hpr = jnp.stack([jnp.concatenate([zeros_h, a_log[l].astype(F32)]),
                         jnp.concatenate([zeros_h, dt_bias[l].astype(F32)])])
        a4, qkvb, gbc, gbr = _project(
            x, norm_mix_w[l][None, :], w_cols(0, o_z), w_cols(o_bg, o_b), wbg, wbg.T,
            conv_qkv_w[l], hpr, hpr.T)
        oa = _gated_delta(a4, gbc, gbr, gdn_norm_w[l][None, :])
        x = _merge_ffn(
            x, oa, qkvb, _rel_bias_row(rel_bias[l]), norm_mix_w[l][None, :], w_cols(o_b, w_l.shape[1]),
            w_branch_a[l].astype(BF16),
            w_branch_b[l].astype(BF16), w_out[l].astype(BF16), norm_ffn_w[l][None, :],
            w_up[l].astype(BF16), conv_ffn_w[l], conv_ffn_b[l][None, :], w_down[l].astype(BF16),
            norm_final_w[None, :], l == depth - 1)
    return x
```

```python
import functools

import jax
import jax.numpy as jnp
from jax import lax
from jax.experimental import pallas as pl
from jax.experimental.pallas import tpu as pltpu

CHUNK = 64
EPS = 1e-6
GDN_HEADS = 4
GDN_DK = 128
GDN_DV = 128
GDN_CONV = 4
ATT_HEADS = 8
ATT_DH = 64
ATT_BAND = 9
REL_CLIP = 128
FFN_CONV = 3

KEY_A = GDN_HEADS * GDN_DK
VAL_A = GDN_HEADS * GDN_DV
WIDTH_B = ATT_HEADS * ATT_DH
CONV_A = 2 * KEY_A + VAL_A
LEAD = (ATT_BAND - 1) * CHUNK
BAND = ATT_BAND * CHUNK

V7X_SUBLANES = 8
V7X_LANES = 128
V7X_VMEM_LIMIT_BYTES = 56 * 1024 * 1024

BIAS_ROLL = -(-(BAND + CHUNK - 1) // V7X_LANES) * V7X_LANES

PROJ_ROWS = 1024
PROJ_COLS = 256
GDN_CHUNKS = 8
GDN_GROUP = 4
FFN_ROWS = 256
FFN_COLS = 256

F32 = jnp.float32
BF16 = jnp.bfloat16
NEG_LOG2_E = -1.4426950408889634


def _dot(a, b):
    return jnp.dot(a.astype(BF16), b.astype(BF16), preferred_element_type=F32)


def _dot_nt(a, b):
    return lax.dot_general(a.astype(BF16), b.astype(BF16), (((1,), (1,)), ((), ())),
                           preferred_element_type=F32)


def _dot_tn(a, b):
    return lax.dot_general(a.astype(BF16), b.astype(BF16), (((0,), (0,)), ((), ())),
                           preferred_element_type=F32)


def _sigmoid(x):
    return 1.0 / (1.0 + jnp.exp2(x * NEG_LOG2_E))


def _softplus(x):
    return jnp.maximum(x, 0.0) + jnp.log1p(jnp.exp(-jnp.abs(x)))


def _rms_scale(x):
    return lax.rsqrt(jnp.mean(x * x, axis=-1, keepdims=True) + EPS)


def _each(fn, *lists):
    return [fn(*args) for args in zip(*lists)]


class _Interleaver:
    def __init__(self, pieces, period, burst=1):
        self.pieces, self.period, self.burst, self.calls = list(pieces), period, burst, 0

    def __call__(self):
        self.calls += 1
        if self.calls % self.period == 0:
            for _ in range(min(self.burst, len(self.pieces))):
                self.pieces.pop(0)()

    def flush(self):
        while self.pieces:
            self.pieces.pop(0)()


def _const_spec(shape):
    zeros = (0,) * len(shape)
    return pl.BlockSpec(shape, lambda *_: zeros, pipeline_mode=pl.Buffered(1))


def _params(n_axes):
    return pltpu.CompilerParams(dimension_semantics=("arbitrary",) * n_axes,
                                vmem_limit_bytes=V7X_VMEM_LIMIT_BYTES)


def _proj_kernel(x_ref, nw_ref, wa_ref, wb_ref, wbg_ref, wbgt_ref, cw_ref, hpr_ref, hpc_ref,
                 a4_ref, qkvb_ref, gbc_ref, gbr_ref, h_ref, cbuf_ref):
    tm = x_ref.shape[1]
    halo = V7X_SUBLANES
    x = x_ref[0]
    h = (x * _rms_scale(x) * nw_ref[...]).astype(BF16)
    h_ref[0] = h

    pc = jnp.dot(h, wbg_ref[...], preferred_element_type=F32)
    lane = lax.broadcasted_iota(jnp.int32, pc.shape, 1)
    gc = -jnp.exp(hpr_ref[0:1, :]) * _softplus(pc + hpr_ref[1:2, :])
    valc = jnp.where(lane < GDN_HEADS, _sigmoid(pc), gc)
    gbc_ref[0] = valc.reshape(tm // CHUNK, CHUNK, 2 * GDN_HEADS)

    pr = lax.dot_general(wbgt_ref[...], h, (((1,), (1,)), ((), ())),
                         preferred_element_type=F32)
    row = lax.broadcasted_iota(jnp.int32, pr.shape, 0)
    gr = -jnp.exp(hpc_ref[:, 0:1]) * _softplus(pr + hpc_ref[:, 1:2])
    valr = jnp.where(row < GDN_HEADS, _sigmoid(pr), gr)
    for c in range(tm // CHUNK):
        gbr_ref[0, c] = valr[:, c * CHUNK:(c + 1) * CHUNK]

    cbuf_ref[0:halo, :] = jnp.where(pl.program_id(1) == 0, 0.0, cbuf_ref[0:halo, :])

    def project_a(blk):
        cols = slice(blk * PROJ_COLS, (blk + 1) * PROJ_COLS)
        cbuf_ref[halo:, cols] = jnp.dot(h, wa_ref[:, cols], preferred_element_type=F32)

    def project_z(blk):
        cols = slice(CONV_A + blk * PROJ_COLS, CONV_A + (blk + 1) * PROJ_COLS)
        a4_ref[0, :, cols] = jnp.dot(h, wa_ref[:, cols], preferred_element_type=F32)

    def project_b(blk):
        cols = slice(blk * PROJ_COLS, (blk + 1) * PROJ_COLS)
        pb = jnp.dot(h, wb_ref[:, cols], preferred_element_type=F32)
        if (blk + 1) * PROJ_COLS <= WIDTH_B:
            pb = pb * (ATT_DH ** -0.5)
        qkvb_ref[0, :, cols] = pb.astype(BF16)

    def conv_head(blk):
        cols = slice(blk * V7X_LANES, (blk + 1) * V7X_LANES)
        conv = cw_ref[GDN_CONV - 1:GDN_CONV, cols] * cbuf_ref[halo:halo + tm, cols]
        for tap in range(GDN_CONV - 1):
            off = halo - (GDN_CONV - 1) + tap
            conv = conv + cw_ref[tap:tap + 1, cols] * cbuf_ref[off:off + tm, cols]
        cbuf_ref[0:halo, cols] = cbuf_ref[tm:tm + halo, cols]
        act = conv * _sigmoid(conv)
        if blk < 2 * GDN_HEADS:
            inv_norm = lax.rsqrt(jnp.sum(act * act, axis=-1, keepdims=True) + EPS)
            act = act * (inv_norm * (GDN_DK ** -0.5) if blk < GDN_HEADS else inv_norm)
        a4_ref[0, :, cols] = act

    n_a = CONV_A // PROJ_COLS
    heads_per_blk = PROJ_COLS // V7X_LANES
    project_a(0)
    for blk in range(n_a):
        if blk + 1 < n_a:
            project_a(blk + 1)
        project_b(blk)
        if blk < VAL_A // PROJ_COLS:
            project_z(blk)
        for sub in range(heads_per_blk):
            conv_head(blk * heads_per_blk + sub)


def _project(x, nw, wa, wb, wbg, wbgt, cw, hpr, hpc):
    b, t, d = x.shape
    tm = PROJ_ROWS
    nc = tm // CHUNK
    return pl.pallas_call(
        _proj_kernel,
        grid=(b, t // tm),
        in_specs=[
            pl.BlockSpec((1, tm, d), lambda bi, i: (bi, i, 0)),
            _const_spec(nw.shape), _const_spec(wa.shape), _const_spec(wb.shape),
            _const_spec(wbg.shape), _const_spec(wbgt.shape), _const_spec(cw.shape),
            _const_spec(hpr.shape), _const_spec(hpc.shape),
        ],
        out_specs=[
            pl.BlockSpec((1, tm, CONV_A + VAL_A), lambda bi, i: (bi, i, 0)),
            pl.BlockSpec((1, tm, 3 * WIDTH_B), lambda bi, i: (bi, i, 0)),
            pl.BlockSpec((1, nc, CHUNK, 2 * GDN_HEADS), lambda bi, i: (bi, i, 0, 0)),
            pl.BlockSpec((1, nc, 2 * GDN_HEADS, CHUNK), lambda bi, i: (bi, i, 0, 0)),
            pl.BlockSpec((1, tm, d), lambda bi, i: (bi, i, 0)),
        ],
        out_shape=[
            jax.ShapeDtypeStruct((b, t, CONV_A + VAL_A), F32),
            jax.ShapeDtypeStruct((b, t, 3 * WIDTH_B), BF16),
            jax.ShapeDtypeStruct((b, t // CHUNK, CHUNK, 2 * GDN_HEADS), F32),
            jax.ShapeDtypeStruct((b, t // CHUNK, 2 * GDN_HEADS, CHUNK), F32),
            jax.ShapeDtypeStruct((b, t, d), BF16),
        ],
        scratch_shapes=[pltpu.VMEM((tm + V7X_SUBLANES, CONV_A), F32)],
        compiler_params=_params(2),
        name="in_proj",
    )(x, nw, wa, wb, wbg, wbgt, cw, hpr, hpc)


def _unit_lower_inverse(a, ii, jj, between):
    same16 = (ii // 16) == (jj // 16)
    same32 = (ii // 32) == (jj // 32)
    eye = jnp.where(ii == jj, 1.0, 0.0).astype(F32)
    ad = _each(lambda x: jnp.where(same16, x, 0.0), a)
    inv = _each(lambda x: eye - x, ad)
    power = _each(_dot, ad, ad)
    between()
    for level in range(3):
        inv = _each(lambda i, pi: i + pi, inv, _each(_dot, power, inv))
        if level < 2:
            power = _each(_dot, power, power)
        between()
    for off in (_each(lambda x: jnp.where(same32 & jnp.logical_not(same16), x, 0.0), a),
                _each(lambda x: jnp.where(same32, 0.0, x), a)):
        t = _each(_dot, off, inv)
        between()
        inv = _each(lambda i, t_: i - t_, inv, _each(_dot, inv, t))
        between()
    return inv


def _exact_dot_01(a, b, ones_on_left):
    x = b if ones_on_left else a
    mask = (a if ones_on_left else b).astype(BF16)
    acc = None
    for _ in range(3):
        piece = x.astype(BF16)
        x = x - piece.astype(F32)
        term = (jnp.dot(mask, piece, preferred_element_type=F32) if ones_on_left
                else jnp.dot(piece, mask, preferred_element_type=F32))
        acc = term if acc is None else acc + term
    return acc


def _gdn_kernel(a4_ref, gbc_ref, gbr_ref, gnw_ref, oa_ref,
                s_ref, wq_ref, uv_ref, pm_ref, kd_ref, sz_ref, gl_ref):
    nb = a4_ref.shape[0]
    nc = gbc_ref.shape[1]
    step = pl.program_id(0)
    slot_w = step % 2
    slot_r = 1 - slot_w
    chains = [(bi, hd) for bi in range(nb) for hd in range(GDN_HEADS)]
    n_ch = len(chains)

    @pl.when(step == 0)
    def _():
        s_ref[...] = jnp.zeros(s_ref.shape, F32)
        for ref in (wq_ref, uv_ref, pm_ref, kd_ref, sz_ref, gl_ref):
            ref[1] = jnp.zeros(ref.shape[1:], ref.dtype)

    ii = lax.broadcasted_iota(jnp.int32, (CHUNK, CHUNK), 0)
    jj = lax.broadcasted_iota(jnp.int32, (CHUNK, CHUNK), 1)
    incl = ii >= jj
    strict = ii > jj
    tri_lo = jnp.where(incl, 1.0, 0.0).astype(F32)
    tri_up = jnp.where(ii <= jj, 1.0, 0.0).astype(F32)
    gnw = gnw_ref[...]

    def recurrence(c):
        rows = pl.ds(pl.multiple_of(c * CHUNK, CHUNK), CHUNK)
        at = [c * n_ch + ch for ch in range(n_ch)]
        st = {}

        def stage_ws():
            st["s"] = [s_ref[ch] for ch in range(n_ch)]
            st["ws"] = [_dot(wq_ref[slot_r, j], s_) for j, s_ in zip(at, st["s"])]

        def stage_update():
            u = [uv_ref[slot_r, j] - ws_[:CHUNK] for j, ws_ in zip(at, st["ws"])]
            pu = [_dot(pm_ref[slot_r, j], u_) for j, u_ in zip(at, u)]
            ku = [_dot_tn(kd_ref[slot_r, j], u_) for j, u_ in zip(at, u)]
            for ch, (j, s_, ku_) in enumerate(zip(at, st["s"], ku)):
                s_ref[ch] = jnp.exp(gl_ref[slot_r, j][0:1, :]) * s_ + ku_
            for (bi, hd), j, ws_, pu_ in zip(chains, at, st["ws"], pu):
                o = ws_[CHUNK:] + pu_
                o = o * _rms_scale(o) * gnw
                oa_ref[bi, rows, hd * GDN_DV:(hd + 1) * GDN_DV] = (o * sz_ref[slot_r, j]).astype(BF16)

        return [stage_ws, stage_update]

    def pair_step(pr, carry):
        cs = [GDN_GROUP * pr + e for e in range(GDN_GROUP)]

        pieces = _Interleaver([piece for c in cs for piece in recurrence(c)],
                              period=max(1, 4 // GDN_GROUP), burst=max(1, GDN_GROUP // 4))

        items = [(e, bi, hd) for e in range(GDN_GROUP) for bi, hd in chains]
        rows = [pl.ds(pl.multiple_of(c * CHUNK, CHUNK), CHUNK) for c in cs]

        gcol = [[_exact_dot_01(tri_lo, gbc_ref[bi, c], ones_on_left=True) for bi in range(nb)] for c in cs]
        grow = [[_exact_dot_01(gbr_ref[bi, c], tri_up, ones_on_left=False) for bi in range(nb)] for c in cs]

        def load(col0):
            return [a4_ref[bi, rows[e], col0 + hd * GDN_DK:col0 + (hd + 1) * GDN_DK] for e, bi, hd in items]

        q, k, v = load(0), load(KEY_A), load(2 * KEY_A)
        g_c = [gcol[e][bi][:, GDN_HEADS + hd:GDN_HEADS + hd + 1] for e, bi, hd in items]
        g_r = [grow[e][bi][GDN_HEADS + hd:GDN_HEADS + hd + 1, :] for e, bi, hd in items]
        beta = [gbc_ref[bi, cs[e]][:, hd:hd + 1] for e, bi, hd in items]
        g_last = [x[CHUNK - 1:CHUNK, :] for x in g_c]
        dec = _each(lambda gc, gr: jnp.exp(jnp.where(incl, gc - gr, 0.0)), g_c, g_r)
        gam = _each(jnp.exp, g_c)

        kq = _each(lambda k_, q_: _dot_nt(jnp.concatenate([k_, q_], axis=0), k_), k, q)
        pieces()
        a_mat = _each(lambda b_, x, d_: jnp.where(strict, b_ * x[:CHUNK] * d_, 0.0), beta, kq, dec)
        p_mat = _each(lambda x, d_: jnp.where(incl, x[CHUNK:] * d_, 0.0), kq, dec)
        inv = _unit_lower_inverse(a_mat, ii, jj, pieces)
        rhs = _each(lambda b_, g_, k_, v_: jnp.concatenate([(b_ * g_) * k_, b_ * v_], axis=1),
                    beta, gam, k, v)
        sol = _each(_dot, inv, rhs)
        pieces.flush()

        z = load(CONV_A)
        for n, (e, bi, hd) in enumerate(items):
            j = cs[e] * n_ch + bi * GDN_HEADS + hd
            wq_ref[slot_w, j] = jnp.concatenate([sol[n][:, :GDN_DK], q[n] * gam[n]], axis=0).astype(BF16)
            uv_ref[slot_w, j] = sol[n][:, GDN_DK:]
            pm_ref[slot_w, j] = p_mat[n].astype(BF16)
            kd_ref[slot_w, j] = (k[n] * jnp.exp(g_last[n] - g_c[n])).astype(BF16)
            sz_ref[slot_w, j] = z[n] * _sigmoid(z[n])
            gl_ref[slot_w, j] = jnp.broadcast_to(g_last[n], gl_ref.shape[2:])
        return carry

    lax.fori_loop(0, nc // GDN_GROUP, pair_step, 0)


def _gated_delta(a4, gbc, gbr, gnw):
    b, t, _ = a4.shape
    nc = GDN_CHUNKS
    rows = nc * CHUNK
    n_tiles = t // rows
    per_tile = nc * b * GDN_HEADS

    def tile_in(i):
        return jnp.minimum(i, n_tiles - 1)

    def tile_out(i):
        return jnp.maximum(i - 1, 0)

    return pl.pallas_call(
        _gdn_kernel,
        grid=(n_tiles + 1,),
        in_specs=[
            pl.BlockSpec((b, rows, CONV_A + VAL_A), lambda i: (0, tile_in(i), 0)),
            pl.BlockSpec((b, nc, CHUNK, 2 * GDN_HEADS), lambda i: (0, tile_in(i), 0, 0)),
            pl.BlockSpec((b, nc, 2 * GDN_HEADS, CHUNK), lambda i: (0, tile_in(i), 0, 0)),
            _const_spec(gnw.shape),
        ],
        out_specs=pl.BlockSpec((b, rows, VAL_A), lambda i: (0, tile_out(i), 0)),
        out_shape=jax.ShapeDtypeStruct((b, t, VAL_A), BF16),
        scratch_shapes=[
            pltpu.VMEM((b * GDN_HEADS, GDN_DK, GDN_DV), F32),
            pltpu.VMEM((2, per_tile, 2 * CHUNK, GDN_DK), BF16),
            pltpu.VMEM((2, per_tile, CHUNK, GDN_DV), F32),
            pltpu.VMEM((2, per_tile, CHUNK, CHUNK), BF16),
            pltpu.VMEM((2, per_tile, CHUNK, GDN_DK), BF16),
            pltpu.VMEM((2, per_tile, CHUNK, GDN_DV), F32),
            pltpu.VMEM((2, per_tile, V7X_SUBLANES, V7X_LANES), F32),
        ],
        compiler_params=_params(1),
        name="gated_delta",
    )(a4, gbc, gbr, gnw)


def _attention_prepare(k_ref, v_ref, rel_ref, kbuf_ref, vbuf_ref, bias_ref):
    tq = k_ref.shape[1]
    i = pl.program_id(1)

    @pl.when((i == 0) & (pl.program_id(0) == 0))
    def _():
        for hd in range(ATT_HEADS):
            rel = jnp.broadcast_to(rel_ref[hd:hd + 1, :], (CHUNK, BIAS_ROLL))
            rolled = pltpu.roll(rel, 0, 1, stride=1, stride_axis=0)
            bias_ref[hd // 2, (hd % 2) * CHUNK:(hd % 2 + 1) * CHUNK, :] = rolled[:, :BAND]

    zeros = jnp.zeros((LEAD, WIDTH_B), BF16)
    kbuf_ref[0:LEAD, :] = jnp.where(i == 0, zeros, kbuf_ref[tq:tq + LEAD, :])
    vbuf_ref[0:LEAD, :] = jnp.where(i == 0, zeros, vbuf_ref[tq:tq + LEAD, :])
    kbuf_ref[LEAD:, :] = k_ref[0]
    vbuf_ref[LEAD:, :] = v_ref[0]


def _attention_chunk(c, q_ref, kbuf_ref, vbuf_ref, bias_ref, ob_ref):
    tq = q_ref.shape[1]
    i = pl.program_id(1)
    rows = slice(c * CHUNK, (c + 1) * CHUNK)
    band = slice(c * CHUNK, c * CHUNK + BAND)
    pairs = list(range(ATT_HEADS // 2))
    lanes = [slice(pr * 2 * ATT_DH, (pr + 1) * 2 * ATT_DH) for pr in pairs]
    first = lax.broadcasted_iota(jnp.int32, (CHUNK, 2 * ATT_DH), 1) < ATT_DH
    st = {}

    def scores():
        key = lax.broadcasted_iota(jnp.int32, (1, BAND), 1)
        valid = (i * tq - LEAD + c * CHUNK + key) >= 0
        zero_q = jnp.zeros((CHUNK, 2 * ATT_DH), BF16)

        def stacked_q(ln):
            q = q_ref[0, rows, ln]
            return jnp.concatenate([jnp.where(first, q, zero_q), jnp.where(first, zero_q, q)], axis=0)

        s = [_dot_nt(stacked_q(ln), kbuf_ref[band, ln]) for ln in lanes]
        s = [jnp.where(valid, x + bias_ref[pr], -jnp.inf) for x, pr in zip(s, pairs)]
        p = [jnp.exp(x - jnp.max(x, axis=-1, keepdims=True)) for x in s]
        st["denom"] = [jnp.sum(x, axis=-1, keepdims=True) for x in p]
        st["p"] = [x.astype(BF16) for x in p]

    def values():
        pv = [_dot(x, vbuf_ref[band, ln]) for x, ln in zip(st["p"], lanes)]
        for x, dn, ln in zip(pv, st["denom"], lanes):
            x = x / dn
            ob_ref[rows, ln] = jnp.where(first, x[:CHUNK], x[CHUNK:]).astype(BF16)

    return [scores, values]


def _ffn_kernel(final_norm, x_ref, h_ref, oa_ref, q_ref, k_ref, v_ref, rel_ref, wg_ref, wa_ref,
                wb_ref, wo_ref, nfw_ref, wup_ref, cfw_ref, cfb_ref, wdn_ref, nlw_ref, out_ref,
                ubuf_ref, kbuf_ref, vbuf_ref, bias_ref, ob_ref):
    tm = x_ref.shape[1]
    d = x_ref.shape[2]
    dff = wdn_ref.shape[0]
    halo = V7X_SUBLANES
    x = x_ref[0]

    _attention_prepare(k_ref, v_ref, rel_ref, kbuf_ref, vbuf_ref, bias_ref)
    ubuf_ref[0:halo, :] = jnp.where(pl.program_id(1) == 0, 0.0, ubuf_ref[0:halo, :])
    attention = _Interleaver(
        [piece for c in range(tm // CHUNK)
         for piece in _attention_chunk(c, q_ref, kbuf_ref, vbuf_ref, bias_ref, ob_ref)], period=1)

    h = h_ref[0]
    gates = []
    for blk in range(2 * d // FFN_COLS):
        gates.append(jnp.dot(h, wg_ref[:, blk * FFN_COLS:(blk + 1) * FFN_COLS], preferred_element_type=F32))
        attention()
    ya = jnp.dot(oa_ref[0], wa_ref[...], preferred_element_type=F32)
    attention.flush()
    gates = jnp.concatenate(gates, axis=1)
    yb = jnp.dot(ob_ref[...], wb_ref[...], preferred_element_type=F32)
    mix = _sigmoid(gates[:, :d]) * ya + _sigmoid(gates[:, d:]) * yb
    x1 = x + _dot(mix, wo_ref[...])

    h2 = (x1 * _rms_scale(x1) * nfw_ref[...]).astype(BF16)

    def gate_up_cols(blk):
        return [slice(half * dff + blk * FFN_COLS, half * dff + (blk + 1) * FFN_COLS) for half in range(2)]

    def project_up(blk):
        for cols in gate_up_cols(blk):
            ubuf_ref[halo:, cols] = jnp.dot(h2, wup_ref[:, cols], preferred_element_type=F32)

    def conv_block(cols):
        acc = cfb_ref[:, cols] + cfw_ref[FFN_CONV - 1:FFN_CONV, cols] * ubuf_ref[halo:halo + tm, cols]
        for tap in range(FFN_CONV - 1):
            off = halo - (FFN_CONV - 1) + tap
            acc = acc + cfw_ref[tap:tap + 1, cols] * ubuf_ref[off:off + tm, cols]
        ubuf_ref[0:halo, cols] = ubuf_ref[tm:tm + halo, cols]
        return acc

    n_blk = dff // FFN_COLS
    for blk in range(n_blk):
        project_up(blk)
    x2 = x1
    for blk in range(n_blk):
        gate, up = [conv_block(cols) for cols in gate_up_cols(blk)]
        act = (gate * _sigmoid(gate) * up).astype(BF16)
        x2 = x2 + jnp.dot(act, wdn_ref[blk * FFN_COLS:(blk + 1) * FFN_COLS, :], preferred_element_type=F32)
    if final_norm:
        x2 = x2 * _rms_scale(x2) * nlw_ref[...]
    out_ref[0] = x2


def _merge_ffn(x, h, oa, qkvb, rel, wg, wa, wb, wo, nfw, wup, cfw, cfb, wdn, nlw, final_norm):
    b, t, d = x.shape
    tm = FFN_ROWS
    dff = wdn.shape[0]
    consts = (rel, wg, wa, wb, wo, nfw, wup, cfw, cfb, wdn, nlw)
    return pl.pallas_call(
        functools.partial(_ffn_kernel, final_norm),
        grid=(b, t // tm),
        in_specs=[
            pl.BlockSpec((1, tm, d), lambda bi, i: (bi, i, 0)),
            pl.BlockSpec((1, tm, d), lambda bi, i: (bi, i, 0)),
            pl.BlockSpec((1, tm, VAL_A), lambda bi, i: (bi, i, 0)),
            pl.BlockSpec((1, tm, WIDTH_B), lambda bi, i: (bi, i, 0)),
            pl.BlockSpec((1, tm, WIDTH_B), lambda bi, i: (bi, i, 1)),
            pl.BlockSpec((1, tm, WIDTH_B), lambda bi, i: (bi, i, 2)),
        ] + [_const_spec(c.shape) for c in consts],
        out_specs=pl.BlockSpec((1, tm, d), lambda bi, i: (bi, i, 0)),
        out_shape=jax.ShapeDtypeStruct((b, t, d), F32),
        scratch_shapes=[
            pltpu.VMEM((tm + V7X_SUBLANES, 2 * dff), F32),
            pltpu.VMEM((LEAD + tm, WIDTH_B), BF16),
            pltpu.VMEM((LEAD + tm, WIDTH_B), BF16),
            pltpu.VMEM((ATT_HEADS // 2, 2 * CHUNK, BAND), F32),
            pltpu.VMEM((tm, WIDTH_B), BF16),
        ],
        compiler_params=_params(2),
        name="merge_ffn",
    )(x, h, oa, qkvb, qkvb, qkvb, *consts)


def _rel_bias_row(rel_table):
    n = jnp.arange(BIAS_ROLL)
    dist = jnp.where(n < BAND, jnp.clip(LEAD - n, -REL_CLIP, REL_CLIP), REL_CLIP)
    return rel_table.astype(F32)[:, dist + REL_CLIP]


def kernel(x, norm_mix_w, w_in, conv_qkv_w, a_log, dt_bias, gdn_norm_w, w_branch_a, w_branch_b,
           rel_bias, w_out, norm_ffn_w, w_up, conv_ffn_w, conv_ffn_b, w_down, norm_final_w):
    depth = w_in.shape[0]
    o_z = CONV_A + VAL_A
    o_bg = o_z + 2 * GDN_HEADS
    o_b = o_bg + 3 * WIDTH_B
    zeros_h = jnp.zeros((GDN_HEADS,), F32)
    for l in range(depth):
        w_l = w_in[l]

        def w_cols(lo, hi, w_l=w_l):
            return w_l[:, lo:hi].astype(BF16)

        wbg = w_cols(o_z, o_bg)
        hpr = jnp.stack([jnp.concatenate([zeros_h, a_log[l].astype(F32)]),
                         jnp.concatenate([zeros_h, dt_bias[l].astype(F32)])])
        a4, qkvb, gbc, gbr, h = _project(
            x, norm_mix_w[l][None, :], w_cols(0, o_z), w_cols(o_bg, o_b), wbg, wbg.T,
            conv_qkv_w[l], hpr, hpr.T)
        oa = _gated_delta(a4, gbc, gbr, gdn_norm_w[l][None, :])
        x = _merge_ffn(
            x, h, oa, qkvb, _rel_bias_row(rel_bias[l]), w_cols(o_b, w_l.shape[1]),
            w_branch_a[l].astype(BF16),
            w_branch_b[l].astype(BF16), w_out[l].astype(BF16), norm_ffn_w[l][None, :],
            w_up[l].astype(BF16), conv_ffn_w[l], conv_ffn_b[l][None, :], w_down[l].astype(BF16),
            norm_final_w[None, :], l == depth - 1)
    return x
```

```python
import functools

import jax
import jax.numpy as jnp
from jax import lax
from jax.experimental import pallas as pl
from jax.experimental.pallas import tpu as pltpu

CHUNK = 64
EPS = 1e-6
GDN_HEADS = 4
GDN_DK = 128
GDN_DV = 128
GDN_CONV = 4
ATT_HEADS = 8
ATT_DH = 64
ATT_BAND = 9
REL_CLIP = 128
FFN_CONV = 3

KEY_A = GDN_HEADS * GDN_DK
VAL_A = GDN_HEADS * GDN_DV
WIDTH_B = ATT_HEADS * ATT_DH
CONV_A = 2 * KEY_A + VAL_A
LEAD = (ATT_BAND - 1) * CHUNK
BAND = ATT_BAND * CHUNK

V7X_SUBLANES = 8
V7X_LANES = 128
V7X_VMEM_LIMIT_BYTES = 56 * 1024 * 1024

BIAS_ROLL = -(-(BAND + CHUNK - 1) // V7X_LANES) * V7X_LANES

PROJ_ROWS = 1024
PROJ_COLS = 256
GDN_CHUNKS = 8
GDN_GROUP = 4
FFN_ROWS = 256
FFN_COLS = 256

F32 = jnp.float32
BF16 = jnp.bfloat16
LOG2_E = 1.4426950408889634
NEG_LOG2_E = -LOG2_E


def _dot(a, b):
    return jnp.dot(a.astype(BF16), b.astype(BF16), preferred_element_type=F32)


def _dot_nt(a, b):
    return lax.dot_general(a.astype(BF16), b.astype(BF16), (((1,), (1,)), ((), ())),
                           preferred_element_type=F32)


def _dot_tn(a, b):
    return lax.dot_general(a.astype(BF16), b.astype(BF16), (((0,), (0,)), ((), ())),
                           preferred_element_type=F32)


def _sigmoid(x):
    return 1.0 / (1.0 + jnp.exp2(x * NEG_LOG2_E))


def _softplus(x):
    return jnp.maximum(x, 0.0) + jnp.log1p(jnp.exp(-jnp.abs(x)))


def _rms_scale(x):
    return lax.rsqrt(jnp.mean(x * x, axis=-1, keepdims=True) + EPS)


def _each(fn, *lists):
    return [fn(*args) for args in zip(*lists)]


class _Interleaver:
    def __init__(self, pieces, period, burst=1):
        self.pieces, self.period, self.burst, self.calls = list(pieces), period, burst, 0

    def __call__(self):
        self.calls += 1
        if self.calls % self.period == 0:
            for _ in range(min(self.burst, len(self.pieces))):
                self.pieces.pop(0)()

    def flush(self):
        while self.pieces:
            self.pieces.pop(0)()


def _const_spec(shape):
    zeros = (0,) * len(shape)
    return pl.BlockSpec(shape, lambda *_: zeros, pipeline_mode=pl.Buffered(1))


def _params(n_axes):
    return pltpu.CompilerParams(dimension_semantics=("arbitrary",) * n_axes,
                                vmem_limit_bytes=V7X_VMEM_LIMIT_BYTES)


def _proj_kernel(x_ref, nw_ref, wa_ref, wb_ref, wbg_ref, wbgt_ref, cw_ref, hpr_ref, hpc_ref,
                 a4_ref, qkvb_ref, gbc_ref, gbr_ref, h_ref, cbuf_ref):
    tm = x_ref.shape[1]
    halo = V7X_SUBLANES
    x = x_ref[0]
    h = (x * _rms_scale(x) * nw_ref[...]).astype(BF16)
    h_ref[0] = h

    pc = jnp.dot(h, wbg_ref[...], preferred_element_type=F32)
    lane = lax.broadcasted_iota(jnp.int32, pc.shape, 1)
    gc = -jnp.exp(hpr_ref[0:1, :]) * _softplus(pc + hpr_ref[1:2, :])
    valc = jnp.where(lane < GDN_HEADS, _sigmoid(pc), gc)
    gbc_ref[0] = valc.reshape(tm // CHUNK, CHUNK, 2 * GDN_HEADS)

    pr = lax.dot_general(wbgt_ref[...], h, (((1,), (1,)), ((), ())),
                         preferred_element_type=F32)
    row = lax.broadcasted_iota(jnp.int32, pr.shape, 0)
    gr = -jnp.exp(hpc_ref[:, 0:1]) * _softplus(pr + hpc_ref[:, 1:2])
    valr = jnp.where(row < GDN_HEADS, _sigmoid(pr), gr)
    for c in range(tm // CHUNK):
        gbr_ref[0, c] = valr[:, c * CHUNK:(c + 1) * CHUNK]

    cbuf_ref[0:halo, :] = jnp.where(pl.program_id(1) == 0, 0.0, cbuf_ref[0:halo, :])

    def project_a(blk):
        cols = slice(blk * PROJ_COLS, (blk + 1) * PROJ_COLS)
        cbuf_ref[halo:, cols] = jnp.dot(h, wa_ref[:, cols], preferred_element_type=F32)

    def project_z(blk):
        cols = slice(CONV_A + blk * PROJ_COLS, CONV_A + (blk + 1) * PROJ_COLS)
        a4_ref[0, :, cols] = jnp.dot(h, wa_ref[:, cols], preferred_element_type=F32)

    def project_b(blk):
        cols = slice(blk * PROJ_COLS, (blk + 1) * PROJ_COLS)
        pb = jnp.dot(h, wb_ref[:, cols], preferred_element_type=F32)
        if (blk + 1) * PROJ_COLS <= WIDTH_B:
            pb = pb * (ATT_DH ** -0.5 * LOG2_E)
        qkvb_ref[0, :, cols] = pb.astype(BF16)

    def conv_head(blk):
        cols = slice(blk * V7X_LANES, (blk + 1) * V7X_LANES)
        conv = cw_ref[GDN_CONV - 1:GDN_CONV, cols] * cbuf_ref[halo:halo + tm, cols]
        for tap in range(GDN_CONV - 1):
            off = halo - (GDN_CONV - 1) + tap
            conv = conv + cw_ref[tap:tap + 1, cols] * cbuf_ref[off:off + tm, cols]
        cbuf_ref[0:halo, cols] = cbuf_ref[tm:tm + halo, cols]
        act = conv * _sigmoid(conv)
        if blk < 2 * GDN_HEADS:
            inv_norm = lax.rsqrt(jnp.sum(act * act, axis=-1, keepdims=True) + EPS)
            act = act * (inv_norm * (GDN_DK ** -0.5) if blk < GDN_HEADS else inv_norm)
        a4_ref[0, :, cols] = act

    n_a = CONV_A // PROJ_COLS
    heads_per_blk = PROJ_COLS // V7X_LANES
    project_a(0)
    for blk in range(n_a):
        if blk + 1 < n_a:
            project_a(blk + 1)
        project_b(blk)
        if blk < VAL_A // PROJ_COLS:
            project_z(blk)
        for sub in range(heads_per_blk):
            conv_head(blk * heads_per_blk + sub)


def _project(x, nw, wa, wb, wbg, wbgt, cw, hpr, hpc):
    b, t, d = x.shape
    tm = PROJ_ROWS
    nc = tm // CHUNK
    return pl.pallas_call(
        _proj_kernel,
        grid=(b, t // tm),
        in_specs=[
            pl.BlockSpec((1, tm, d), lambda bi, i: (bi, i, 0)),
            _const_spec(nw.shape), _const_spec(wa.shape), _const_spec(wb.shape),
            _const_spec(wbg.shape), _const_spec(wbgt.shape), _const_spec(cw.shape),
            _const_spec(hpr.shape), _const_spec(hpc.shape),
        ],
        out_specs=[
            pl.BlockSpec((1, tm, CONV_A + VAL_A), lambda bi, i: (bi, i, 0)),
            pl.BlockSpec((1, tm, 3 * WIDTH_B), lambda bi, i: (bi, i, 0)),
            pl.BlockSpec((1, nc, CHUNK, 2 * GDN_HEADS), lambda bi, i: (bi, i, 0, 0)),
            pl.BlockSpec((1, nc, 2 * GDN_HEADS, CHUNK), lambda bi, i: (bi, i, 0, 0)),
            pl.BlockSpec((1, tm, d), lambda bi, i: (bi, i, 0)),
        ],
        out_shape=[
            jax.ShapeDtypeStruct((b, t, CONV_A + VAL_A), F32),
            jax.ShapeDtypeStruct((b, t, 3 * WIDTH_B), BF16),
            jax.ShapeDtypeStruct((b, t // CHUNK, CHUNK, 2 * GDN_HEADS), F32),
            jax.ShapeDtypeStruct((b, t // CHUNK, 2 * GDN_HEADS, CHUNK), F32),
            jax.ShapeDtypeStruct((b, t, d), BF16),
        ],
        scratch_shapes=[pltpu.VMEM((tm + V7X_SUBLANES, CONV_A), F32)],
        compiler_params=_params(2),
        name="in_proj",
    )(x, nw, wa, wb, wbg, wbgt, cw, hpr, hpc)


def _unit_lower_inverse(a, ii, jj, between):
    same16 = (ii // 16) == (jj // 16)
    same32 = (ii // 32) == (jj // 32)
    eye = jnp.where(ii == jj, 1.0, 0.0).astype(F32)
    ad = _each(lambda x: jnp.where(same16, x, 0.0), a)
    inv = _each(lambda x: eye - x, ad)
    power = _each(_dot, ad, ad)
    between()
    for level in range(3):
        inv = _each(lambda i, pi: i + pi, inv, _each(_dot, power, inv))
        if level < 2:
            power = _each(_dot, power, power)
        between()
    for off in (_each(lambda x: jnp.where(same32 & jnp.logical_not(same16), x, 0.0), a),
                _each(lambda x: jnp.where(same32, 0.0, x), a)):
        t = _each(_dot, off, inv)
        between()
        inv = _each(lambda i, t_: i - t_, inv, _each(_dot, inv, t))
        between()
    return inv


def _exact_dot_01(a, b, ones_on_left):
    x = b if ones_on_left else a
    mask = (a if ones_on_left else b).astype(BF16)
    acc = None
    for _ in range(3):
        piece = x.astype(BF16)
        x = x - piece.astype(F32)
        term = (jnp.dot(mask, piece, preferred_element_type=F32) if ones_on_left
                else jnp.dot(piece, mask, preferred_element_type=F32))
        acc = term if acc is None else acc + term
    return acc


def _gdn_kernel(a4_ref, gbc_ref, gbr_ref, gnw_ref, oa_ref,
                s_ref, wq_ref, uv_ref, pm_ref, kd_ref, sz_ref, gl_ref):
    nb = a4_ref.shape[0]
    nc = gbc_ref.shape[1]
    step = pl.program_id(0)
    slot_w = step % 2
    slot_r = 1 - slot_w
    chains = [(bi, hd) for bi in range(nb) for hd in range(GDN_HEADS)]
    n_ch = len(chains)

    @pl.when(step == 0)
    def _():
        s_ref[...] = jnp.zeros(s_ref.shape, F32)
        for ref in (wq_ref, uv_ref, pm_ref, kd_ref, sz_ref, gl_ref):
            ref[1] = jnp.zeros(ref.shape[1:], ref.dtype)

    ii = lax.broadcasted_iota(jnp.int32, (CHUNK, CHUNK), 0)
    jj = lax.broadcasted_iota(jnp.int32, (CHUNK, CHUNK), 1)
    incl = ii >= jj
    strict = ii > jj
    tri_lo = jnp.where(incl, 1.0, 0.0).astype(F32)
    tri_up = jnp.where(ii <= jj, 1.0, 0.0).astype(F32)
    gnw = gnw_ref[...]

    def recurrence(c):
        rows = pl.ds(pl.multiple_of(c * CHUNK, CHUNK), CHUNK)
        at = [c * n_ch + ch for ch in range(n_ch)]
        st = {}

        def stage_ws():
            st["s"] = [s_ref[ch] for ch in range(n_ch)]
            st["ws"] = [_dot(wq_ref[slot_r, j], s_) for j, s_ in zip(at, st["s"])]

        def stage_update():
            u = [uv_ref[slot_r, j] - ws_[:CHUNK] for j, ws_ in zip(at, st["ws"])]
            pu = [_dot(pm_ref[slot_r, j], u_) for j, u_ in zip(at, u)]
            ku = [_dot_tn(kd_ref[slot_r, j], u_) for j, u_ in zip(at, u)]
            for ch, (j, s_, ku_) in enumerate(zip(at, st["s"], ku)):
                s_ref[ch] = jnp.exp(gl_ref[slot_r, j][0:1, :]) * s_ + ku_
            for (bi, hd), j, ws_, pu_ in zip(chains, at, st["ws"], pu):
                o = ws_[CHUNK:] + pu_
                o = o * _rms_scale(o) * gnw
                oa_ref[bi, rows, hd * GDN_DV:(hd + 1) * GDN_DV] = (o * sz_ref[slot_r, j]).astype(BF16)

        return [stage_ws, stage_update]

    def pair_step(pr, carry):
        cs = [GDN_GROUP * pr + e for e in range(GDN_GROUP)]

        pieces = _Interleaver([piece for c in cs for piece in recurrence(c)],
                              period=max(1, 4 // GDN_GROUP), burst=max(1, GDN_GROUP // 4))

        items = [(e, bi, hd) for e in range(GDN_GROUP) for bi, hd in chains]
        rows = [pl.ds(pl.multiple_of(c * CHUNK, CHUNK), CHUNK) for c in cs]

        gcol = [[_exact_dot_01(tri_lo, gbc_ref[bi, c], ones_on_left=True) for bi in range(nb)] for c in cs]
        grow = [[_exact_dot_01(gbr_ref[bi, c], tri_up, ones_on_left=False) for bi in range(nb)] for c in cs]

        def load(col0):
            return [a4_ref[bi, rows[e], col0 + hd * GDN_DK:col0 + (hd + 1) * GDN_DK] for e, bi, hd in items]

        q, k, v = load(0), load(KEY_A), load(2 * KEY_A)
        g_c = [gcol[e][bi][:, GDN_HEADS + hd:GDN_HEADS + hd + 1] for e, bi, hd in items]
        g_r = [grow[e][bi][GDN_HEADS + hd:GDN_HEADS + hd + 1, :] for e, bi, hd in items]
        beta = [gbc_ref[bi, cs[e]][:, hd:hd + 1] for e, bi, hd in items]
        g_last = [x[CHUNK - 1:CHUNK, :] for x in g_c]
        dec = _each(lambda gc, gr: jnp.exp(jnp.where(incl, gc - gr, 0.0)), g_c, g_r)
        gam = _each(jnp.exp, g_c)

        kq = _each(lambda k_, q_: _dot_nt(jnp.concatenate([k_, q_], axis=0), k_), k, q)
        pieces()
        a_mat = _each(lambda b_, x, d_: jnp.where(strict, b_ * x[:CHUNK] * d_, 0.0), beta, kq, dec)
        p_mat = _each(lambda x, d_: jnp.where(incl, x[CHUNK:] * d_, 0.0), kq, dec)
        inv = _unit_lower_inverse(a_mat, ii, jj, pieces)
        rhs = _each(lambda b_, g_, k_, v_: jnp.concatenate([(b_ * g_) * k_, b_ * v_], axis=1),
                    beta, gam, k, v)
        sol = _each(_dot, inv, rhs)
        pieces.flush()

        z = load(CONV_A)
        for n, (e, bi, hd) in enumerate(items):
            j = cs[e] * n_ch + bi * GDN_HEADS + hd
            wq_ref[slot_w, j] = jnp.concatenate([sol[n][:, :GDN_DK], q[n] * gam[n]], axis=0).astype(BF16)
            uv_ref[slot_w, j] = sol[n][:, GDN_DK:]
            pm_ref[slot_w, j] = p_mat[n].astype(BF16)
            kd_ref[slot_w, j] = (k[n] * jnp.exp(g_last[n] - g_c[n])).astype(BF16)
            sz_ref[slot_w, j] = z[n] * _sigmoid(z[n])
            gl_ref[slot_w, j] = jnp.broadcast_to(g_last[n], gl_ref.shape[2:])
        return carry

    lax.fori_loop(0, nc // GDN_GROUP, pair_step, 0)


def _gated_delta(a4, gbc, gbr, gnw):
    b, t, _ = a4.shape
    nc = GDN_CHUNKS
    rows = nc * CHUNK
    n_tiles = t // rows
    per_tile = nc * b * GDN_HEADS

    def tile_in(i):
        return jnp.minimum(i, n_tiles - 1)

    def tile_out(i):
        return jnp.maximum(i - 1, 0)

    return pl.pallas_call(
        _gdn_kernel,
        grid=(n_tiles + 1,),
        in_specs=[
            pl.BlockSpec((b, rows, CONV_A + VAL_A), lambda i: (0, tile_in(i), 0)),
            pl.BlockSpec((b, nc, CHUNK, 2 * GDN_HEADS), lambda i: (0, tile_in(i), 0, 0)),
            pl.BlockSpec((b, nc, 2 * GDN_HEADS, CHUNK), lambda i: (0, tile_in(i), 0, 0)),
            _const_spec(gnw.shape),
        ],
        out_specs=pl.BlockSpec((b, rows, VAL_A), lambda i: (0, tile_out(i), 0)),
        out_shape=jax.ShapeDtypeStruct((b, t, VAL_A), BF16),
        scratch_shapes=[
            pltpu.VMEM((b * GDN_HEADS, GDN_DK, GDN_DV), F32),
            pltpu.VMEM((2, per_tile, 2 * CHUNK, GDN_DK), BF16),
            pltpu.VMEM((2, per_tile, CHUNK, GDN_DV), F32),
            pltpu.VMEM((2, per_tile, CHUNK, CHUNK), BF16),
            pltpu.VMEM((2, per_tile, CHUNK, GDN_DK), BF16),
            pltpu.VMEM((2, per_tile, CHUNK, GDN_DV), F32),
            pltpu.VMEM((2, per_tile, V7X_SUBLANES, V7X_LANES), F32),
        ],
        compiler_params=_params(1),
        name="gated_delta",
    )(a4, gbc, gbr, gnw)


def _attention_prepare(k_ref, v_ref, rel_ref, kbuf_ref, vbuf_ref, bias_ref):
    tq = k_ref.shape[1]
    i = pl.program_id(1)

    @pl.when((i == 0) & (pl.program_id(0) == 0))
    def _():
        for hd in range(ATT_HEADS):
            rel = jnp.broadcast_to(rel_ref[hd:hd + 1, :] * LOG2_E, (CHUNK, BIAS_ROLL))
            rolled = pltpu.roll(rel, 0, 1, stride=1, stride_axis=0)
            bias_ref[hd // 2, (hd % 2) * CHUNK:(hd % 2 + 1) * CHUNK, :] = rolled[:, :BAND]

    zeros = jnp.zeros((LEAD, WIDTH_B), BF16)
    kbuf_ref[0:LEAD, :] = jnp.where(i == 0, zeros, kbuf_ref[tq:tq + LEAD, :])
    vbuf_ref[0:LEAD, :] = jnp.where(i == 0, zeros, vbuf_ref[tq:tq + LEAD, :])
    kbuf_ref[LEAD:, :] = k_ref[0]
    vbuf_ref[LEAD:, :] = v_ref[0]


def _attention_chunk(c, q_ref, kbuf_ref, vbuf_ref, bias_ref, ob_ref):
    tq = q_ref.shape[1]
    i = pl.program_id(1)
    rows = slice(c * CHUNK, (c + 1) * CHUNK)
    band = slice(c * CHUNK, c * CHUNK + BAND)
    pairs = list(range(ATT_HEADS // 2))
    lanes = [slice(pr * 2 * ATT_DH, (pr + 1) * 2 * ATT_DH) for pr in pairs]
    first = lax.broadcasted_iota(jnp.int32, (CHUNK, 2 * ATT_DH), 1) < ATT_DH
    st = {}

    def scores():
        key = lax.broadcasted_iota(jnp.int32, (1, BAND), 1)
        valid = (i * tq - LEAD + c * CHUNK + key) >= 0
        zero_q = jnp.zeros((CHUNK, 2 * ATT_DH), BF16)

        def stacked_q(ln):
            q = q_ref[0, rows, ln]
            return jnp.concatenate([jnp.where(first, q, zero_q), jnp.where(first, zero_q, q)], axis=0)

        s = [_dot_nt(stacked_q(ln), kbuf_ref[band, ln]) for ln in lanes]
        s = [jnp.where(valid, x + bias_ref[pr], -jnp.inf) for x, pr in zip(s, pairs)]
        p = [jnp.exp2(x - jnp.max(x, axis=-1, keepdims=True)) for x in s]
        st["denom"] = [jnp.sum(x, axis=-1, keepdims=True) for x in p]
        st["p"] = [x.astype(BF16) for x in p]

    def values():
        pv = [_dot(x, vbuf_ref[band, ln]) for x, ln in zip(st["p"], lanes)]
        for x, dn, ln in zip(pv, st["denom"], lanes):
            x = x / dn
            ob_ref[rows, ln] = jnp.where(first, x[:CHUNK], x[CHUNK:]).astype(BF16)

    return [scores, values]


def _ffn_kernel(final_norm, x_ref, h_ref, oa_ref, q_ref, k_ref, v_ref, rel_ref, wg_ref, wa_ref,
                wb_ref, wo_ref, nfw_ref, wup_ref, cfw_ref, cfb_ref, wdn_ref, nlw_ref, out_ref,
                ubuf_ref, kbuf_ref, vbuf_ref, bias_ref, ob_ref):
    tm = x_ref.shape[1]
    d = x_ref.shape[2]
    dff = wdn_ref.shape[0]
    halo = V7X_SUBLANES
    x = x_ref[0]

    _attention_prepare(k_ref, v_ref, rel_ref, kbuf_ref, vbuf_ref, bias_ref)
    ubuf_ref[0:halo, :] = jnp.where(pl.program_id(1) == 0, 0.0, ubuf_ref[0:halo, :])
    attention = _Interleaver(
        [piece for c in range(tm // CHUNK)
         for piece in _attention_chunk(c, q_ref, kbuf_ref, vbuf_ref, bias_ref, ob_ref)], period=1)

    h = h_ref[0]
    gates = []
    for blk in range(2 * d // FFN_COLS):
        gates.append(jnp.dot(h, wg_ref[:, blk * FFN_COLS:(blk + 1) * FFN_COLS], preferred_element_type=F32))
        attention()
    ya = jnp.dot(oa_ref[0], wa_ref[...], preferred_element_type=F32)
    attention.flush()
    gates = jnp.concatenate(gates, axis=1)
    yb = jnp.dot(ob_ref[...], wb_ref[...], preferred_element_type=F32)
    mix = _sigmoid(gates[:, :d]) * ya + _sigmoid(gates[:, d:]) * yb
    x1 = x + _dot(mix, wo_ref[...])

    h2 = (x1 * _rms_scale(x1) * nfw_ref[...]).astype(BF16)

    def gate_up_cols(blk):
        return [slice(half * dff + blk * FFN_COLS, half * dff + (blk + 1) * FFN_COLS) for half in range(2)]

    def project_up(blk):
        for cols in gate_up_cols(blk):
            ubuf_ref[halo:, cols] = jnp.dot(h2, wup_ref[:, cols], preferred_element_type=F32)

    def conv_block(cols):
        acc = cfb_ref[:, cols] + cfw_ref[FFN_CONV - 1:FFN_CONV, cols] * ubuf_ref[halo:halo + tm, cols]
        for tap in range(FFN_CONV - 1):
            off = halo - (FFN_CONV - 1) + tap
            acc = acc + cfw_ref[tap:tap + 1, cols] * ubuf_ref[off:off + tm, cols]
        ubuf_ref[0:halo, cols] = ubuf_ref[tm:tm + halo, cols]
        return acc

    n_blk = dff // FFN_COLS
    for blk in range(n_blk):
        project_up(blk)
    x2 = x1
    for blk in range(n_blk):
        gate, up = [conv_block(cols) for cols in gate_up_cols(blk)]
        act = (gate * _sigmoid(gate) * up).astype(BF16)
        x2 = x2 + jnp.dot(act, wdn_ref[blk * FFN_COLS:(blk + 1) * FFN_COLS, :], preferred_element_type=F32)
    if final_norm:
        x2 = x2 * _rms_scale(x2) * nlw_ref[...]
    out_ref[0] = x2


def _merge_ffn(x, h, oa, qkvb, rel, wg, wa, wb, wo, nfw, wup, cfw, cfb, wdn, nlw, final_norm):
    b, t, d = x.shape
    tm = FFN_ROWS
    dff = wdn.shape[0]
    consts = (rel, wg, wa, wb, wo, nfw, wup, cfw, cfb, wdn, nlw)
    return pl.pallas_call(
        functools.partial(_ffn_kernel, final_norm),
        grid=(b, t // tm),
        in_specs=[
            pl.BlockSpec((1, tm, d), lambda bi, i: (bi, i, 0)),
            pl.BlockSpec((1, tm, d), lambda bi, i: (bi, i, 0)),
            pl.BlockSpec((1, tm, VAL_A), lambda bi, i: (bi, i, 0)),
            pl.BlockSpec((1, tm, WIDTH_B), lambda bi, i: (bi, i, 0)),
            pl.BlockSpec((1, tm, WIDTH_B), lambda bi, i: (bi, i, 1)),
            pl.BlockSpec((1, tm, WIDTH_B), lambda bi, i: (bi, i, 2)),
        ] + [_const_spec(c.shape) for c in consts],
        out_specs=pl.BlockSpec((1, tm, d), lambda bi, i: (bi, i, 0)),
        out_shape=jax.ShapeDtypeStruct((b, t, d), F32),
        scratch_shapes=[
            pltpu.VMEM((tm + V7X_SUBLANES, 2 * dff), F32),
            pltpu.VMEM((LEAD + tm, WIDTH_B), BF16),
            pltpu.VMEM((LEAD + tm, WIDTH_B), BF16),
            pltpu.VMEM((ATT_HEADS // 2, 2 * CHUNK, BAND), F32),
            pltpu.VMEM((tm, WIDTH_B), BF16),
        ],
        compiler_params=_params(2),
        name="merge_ffn",
    )(x, h, oa, qkvb, qkvb, qkvb, *consts)


def _rel_bias_row(rel_table):
    n = jnp.arange(BIAS_ROLL)
    dist = jnp.where(n < BAND, jnp.clip(LEAD - n, -REL_CLIP, REL_CLIP), REL_CLIP)
    return rel_table.astype(F32)[:, dist + REL_CLIP]


def kernel(x, norm_mix_w, w_in, conv_qkv_w, a_log, dt_bias, gdn_norm_w, w_branch_a, w_branch_b,
           rel_bias, w_out, norm_ffn_w, w_up, conv_ffn_w, conv_ffn_b, w_down, norm_final_w):
    depth = w_in.shape[0]
    o_z = CONV_A + VAL_A
    o_bg = o_z + 2 * GDN_HEADS
    o_b = o_bg + 3 * WIDTH_B
    zeros_h = jnp.zeros((GDN_HEADS,), F32)
    for l in range(depth):
        w_l = w_in[l]

        def w_cols(lo, hi, w_l=w_l):
            return w_l[:, lo:hi].astype(BF16)

        wbg = w_cols(o_z, o_bg)
        hpr = jnp.stack([jnp.concatenate([zeros_h, a_log[l].astype(F32)]),
                         jnp.concatenate([zeros_h, dt_bias[l].astype(F32)])])
        a4, qkvb, gbc, gbr, h = _project(
            x, norm_mix_w[l][None, :], w_cols(0, o_z), w_cols(o_bg, o_b), wbg, wbg.T,
            conv_qkv_w[l], hpr, hpr.T)
        oa = _gated_delta(a4, gbc, gbr, gdn_norm_w[l][None, :])
        x = _merge_ffn(
            x, h, oa, qkvb, _rel_bias_row(rel_bias[l]), w_cols(o_b, w_l.shape[1]),
            w_branch_a[l].astype(BF16),
            w_branch_b[l].astype(BF16), w_out[l].astype(BF16), norm_ffn_w[l][None, :],
            w_up[l].astype(BF16), conv_ffn_w[l], conv_ffn_b[l][None, :], w_down[l].astype(BF16),
            norm_final_w[None, :], l == depth - 1)
    return x
```

```python
import functools

import jax
import jax.numpy as jnp
from jax import lax
from jax.experimental import pallas as pl
from jax.experimental.pallas import tpu as pltpu

CHUNK = 64
EPS = 1e-6
GDN_HEADS = 4
GDN_DK = 128
GDN_DV = 128
GDN_CONV = 4
ATT_HEADS = 8
ATT_DH = 64
ATT_BAND = 9
REL_CLIP = 128
FFN_CONV = 3

KEY_A = GDN_HEADS * GDN_DK
VAL_A = GDN_HEADS * GDN_DV
WIDTH_B = ATT_HEADS * ATT_DH
CONV_A = 2 * KEY_A + VAL_A
LEAD = (ATT_BAND - 1) * CHUNK
BAND = ATT_BAND * CHUNK

V7X_SUBLANES = 8
V7X_LANES = 128
V7X_VMEM_LIMIT_BYTES = 56 * 1024 * 1024

BIAS_ROLL = -(-(BAND + CHUNK - 1) // V7X_LANES) * V7X_LANES

PROJ_ROWS = 1024
PROJ_COLS = 256
GDN_CHUNKS = 8
GDN_GROUP = 4
FFN_ROWS = 256
FFN_COLS = 256

F32 = jnp.float32
BF16 = jnp.bfloat16
LOG2_E = 1.4426950408889634
NEG_LOG2_E = -LOG2_E


def _dot(a, b):
    return jnp.dot(a.astype(BF16), b.astype(BF16), preferred_element_type=F32)


def _dot_nt(a, b):
    return lax.dot_general(a.astype(BF16), b.astype(BF16), (((1,), (1,)), ((), ())),
                           preferred_element_type=F32)


def _dot_tn(a, b):
    return lax.dot_general(a.astype(BF16), b.astype(BF16), (((0,), (0,)), ((), ())),
                           preferred_element_type=F32)


def _sigmoid(x):
    return 1.0 / (1.0 + jnp.exp2(x * NEG_LOG2_E))


def _softplus(x):
    return jnp.maximum(x, 0.0) + jnp.log1p(jnp.exp(-jnp.abs(x)))


def _rms_scale(x):
    return lax.rsqrt(jnp.mean(x * x, axis=-1, keepdims=True) + EPS)


def _each(fn, *lists):
    return [fn(*args) for args in zip(*lists)]


class _Interleaver:
    def __init__(self, pieces, period, burst=1):
        self.pieces, self.period, self.burst, self.calls = list(pieces), period, burst, 0

    def __call__(self):
        self.calls += 1
        if self.calls % self.period == 0:
            for _ in range(min(self.burst, len(self.pieces))):
                self.pieces.pop(0)()

    def flush(self):
        while self.pieces:
            self.pieces.pop(0)()


def _const_spec(shape):
    zeros = (0,) * len(shape)
    return pl.BlockSpec(shape, lambda *_: zeros, pipeline_mode=pl.Buffered(1))


def _params(n_axes):
    return pltpu.CompilerParams(dimension_semantics=("arbitrary",) * n_axes,
                                vmem_limit_bytes=V7X_VMEM_LIMIT_BYTES)


def _proj_kernel(x_ref, nw_ref, wa_ref, wb_ref, wbg_ref, wbgt_ref, cw_ref, hpr_ref, hpc_ref,
                 a4_ref, qkvb_ref, gbc_ref, gbr_ref, h_ref, cbuf_ref):
    tm = x_ref.shape[1]
    halo = V7X_SUBLANES
    x = x_ref[0]
    h = (x * _rms_scale(x) * nw_ref[...]).astype(BF16)
    h_ref[0] = h

    pc = jnp.dot(h, wbg_ref[...], preferred_element_type=F32)
    lane = lax.broadcasted_iota(jnp.int32, pc.shape, 1)
    gc = -jnp.exp(hpr_ref[0:1, :]) * _softplus(pc + hpr_ref[1:2, :])
    valc = jnp.where(lane < GDN_HEADS, _sigmoid(pc), gc)
    gbc_ref[0] = valc.reshape(tm // CHUNK, CHUNK, 2 * GDN_HEADS)

    pr = lax.dot_general(wbgt_ref[...], h, (((1,), (1,)), ((), ())),
                         preferred_element_type=F32)
    row = lax.broadcasted_iota(jnp.int32, pr.shape, 0)
    gr = -jnp.exp(hpc_ref[:, 0:1]) * _softplus(pr + hpc_ref[:, 1:2])
    valr = jnp.where(row < GDN_HEADS, _sigmoid(pr), gr)
    for c in range(tm // CHUNK):
        gbr_ref[0, c] = valr[:, c * CHUNK:(c + 1) * CHUNK]

    cbuf_ref[0:halo, :] = jnp.where(pl.program_id(1) == 0, 0.0, cbuf_ref[0:halo, :])

    def project_a(blk):
        cols = slice(blk * PROJ_COLS, (blk + 1) * PROJ_COLS)
        cbuf_ref[halo:, cols] = jnp.dot(h, wa_ref[:, cols], preferred_element_type=F32)

    def project_z(blk):
        cols = slice(CONV_A + blk * PROJ_COLS, CONV_A + (blk + 1) * PROJ_COLS)
        a4_ref[0, :, cols] = jnp.dot(h, wa_ref[:, cols], preferred_element_type=F32)

    def project_b(blk):
        cols = slice(blk * PROJ_COLS, (blk + 1) * PROJ_COLS)
        pb = jnp.dot(h, wb_ref[:, cols], preferred_element_type=F32)
        if (blk + 1) * PROJ_COLS <= WIDTH_B:
            pb = pb * (ATT_DH ** -0.5 * LOG2_E)
        qkvb_ref[0, :, cols] = pb.astype(BF16)

    def conv_head(blk):
        cols = slice(blk * V7X_LANES, (blk + 1) * V7X_LANES)
        conv = cw_ref[GDN_CONV - 1:GDN_CONV, cols] * cbuf_ref[halo:halo + tm, cols]
        for tap in range(GDN_CONV - 1):
            off = halo - (GDN_CONV - 1) + tap
            conv = conv + cw_ref[tap:tap + 1, cols] * cbuf_ref[off:off + tm, cols]
        cbuf_ref[0:halo, cols] = cbuf_ref[tm:tm + halo, cols]
        act = conv * _sigmoid(conv)
        if blk < 2 * GDN_HEADS:
            inv_norm = lax.rsqrt(jnp.sum(act * act, axis=-1, keepdims=True) + EPS)
            act = act * (inv_norm * (GDN_DK ** -0.5) if blk < GDN_HEADS else inv_norm)
        a4_ref[0, :, cols] = act

    n_a = CONV_A // PROJ_COLS
    heads_per_blk = PROJ_COLS // V7X_LANES
    project_a(0)
    for blk in range(n_a):
        if blk + 1 < n_a:
            project_a(blk + 1)
        project_b(blk)
        if blk < VAL_A // PROJ_COLS:
            project_z(blk)
        for sub in range(heads_per_blk):
            conv_head(blk * heads_per_blk + sub)


def _project(x, nw, wa, wb, wbg, wbgt, cw, hpr, hpc):
    b, t, d = x.shape
    tm = PROJ_ROWS
    nc = tm // CHUNK
    return pl.pallas_call(
        _proj_kernel,
        grid=(b, t // tm),
        in_specs=[
            pl.BlockSpec((1, tm, d), lambda bi, i: (bi, i, 0)),
            _const_spec(nw.shape), _const_spec(wa.shape), _const_spec(wb.shape),
            _const_spec(wbg.shape), _const_spec(wbgt.shape), _const_spec(cw.shape),
            _const_spec(hpr.shape), _const_spec(hpc.shape),
        ],
        out_specs=[
            pl.BlockSpec((1, tm, CONV_A + VAL_A), lambda bi, i: (bi, i, 0)),
            pl.BlockSpec((1, tm, 3 * WIDTH_B), lambda bi, i: (bi, i, 0)),
            pl.BlockSpec((1, nc, CHUNK, 2 * GDN_HEADS), lambda bi, i: (bi, i, 0, 0)),
            pl.BlockSpec((1, nc, 2 * GDN_HEADS, CHUNK), lambda bi, i: (bi, i, 0, 0)),
            pl.BlockSpec((1, tm, d), lambda bi, i: (bi, i, 0)),
        ],
        out_shape=[
            jax.ShapeDtypeStruct((b, t, CONV_A + VAL_A), F32),
            jax.ShapeDtypeStruct((b, t, 3 * WIDTH_B), BF16),
            jax.ShapeDtypeStruct((b, t // CHUNK, CHUNK, 2 * GDN_HEADS), F32),
            jax.ShapeDtypeStruct((b, t // CHUNK, 2 * GDN_HEADS, CHUNK), F32),
            jax.ShapeDtypeStruct((b, t, d), BF16),
        ],
        scratch_shapes=[pltpu.VMEM((tm + V7X_SUBLANES, CONV_A), F32)],
        compiler_params=_params(2),
        name="in_proj",
    )(x, nw, wa, wb, wbg, wbgt, cw, hpr, hpc)


def _unit_lower_inverse(a, ii, jj, between):
    same16 = (ii // 16) == (jj // 16)
    same32 = (ii // 32) == (jj // 32)
    eye = jnp.where(ii == jj, 1.0, 0.0).astype(F32)
    ad = _each(lambda x: jnp.where(same16, x, 0.0), a)
    inv = _each(lambda x: eye - x, ad)
    power = _each(_dot, ad, ad)
    between()
    for level in range(3):
        inv = _each(lambda i, pi: i + pi, inv, _each(_dot, power, inv))
        if level < 2:
            power = _each(_dot, power, power)
        between()
    for off in (_each(lambda x: jnp.where(same32 & jnp.logical_not(same16), x, 0.0), a),
                _each(lambda x: jnp.where(same32, 0.0, x), a)):
        t = _each(_dot, off, inv)
        between()
        inv = _each(lambda i, t_: i - t_, inv, _each(_dot, inv, t))
        between()
    return inv


def _exact_dot_01(a, b, ones_on_left):
    x = b if ones_on_left else a
    mask = (a if ones_on_left else b).astype(BF16)
    acc = None
    for _ in range(3):
        piece = x.astype(BF16)
        x = x - piece.astype(F32)
        term = (jnp.dot(mask, piece, preferred_element_type=F32) if ones_on_left
                else jnp.dot(piece, mask, preferred_element_type=F32))
        acc = term if acc is None else acc + term
    return acc


def _gdn_kernel(a4_ref, gbc_ref, gbr_ref, gnw_ref, oa_ref,
                s_ref, wq_ref, uv_ref, pm_ref, kd_ref, sz_ref, gl_ref):
    nb = a4_ref.shape[0]
    nc = gbc_ref.shape[1]
    step = pl.program_id(0)
    slot_w = step % 2
    slot_r = 1 - slot_w
    chains = [(bi, hd) for bi in range(nb) for hd in range(GDN_HEADS)]
    n_ch = len(chains)

    @pl.when(step == 0)
    def _():
        s_ref[...] = jnp.zeros(s_ref.shape, F32)
        for ref in (wq_ref, uv_ref, pm_ref, kd_ref, sz_ref, gl_ref):
            ref[1] = jnp.zeros(ref.shape[1:], ref.dtype)

    ii = lax.broadcasted_iota(jnp.int32, (CHUNK, CHUNK), 0)
    jj = lax.broadcasted_iota(jnp.int32, (CHUNK, CHUNK), 1)
    incl = ii >= jj
    strict = ii > jj
    tri_lo = jnp.where(incl, 1.0, 0.0).astype(F32)
    tri_up = jnp.where(ii <= jj, 1.0, 0.0).astype(F32)
    gnw = gnw_ref[...]

    def recurrence(c):
        rows = pl.ds(pl.multiple_of(c * CHUNK, CHUNK), CHUNK)
        at = [c * n_ch + ch for ch in range(n_ch)]
        st = {}

        def stage_ws():
            st["s"] = [s_ref[ch] for ch in range(n_ch)]
            st["ws"] = [_dot(wq_ref[slot_r, j], s_) for j, s_ in zip(at, st["s"])]

        def stage_update():
            u = [uv_ref[slot_r, j] - ws_[:CHUNK] for j, ws_ in zip(at, st["ws"])]
            pu = [_dot(pm_ref[slot_r, j], u_) for j, u_ in zip(at, u)]
            ku = [_dot_tn(kd_ref[slot_r, j], u_) for j, u_ in zip(at, u)]
            for ch, (j, s_, ku_) in enumerate(zip(at, st["s"], ku)):
                s_ref[ch] = jnp.exp(gl_ref[slot_r, j][0:1, :]) * s_ + ku_
            for (bi, hd), j, ws_, pu_ in zip(chains, at, st["ws"], pu):
                o = ws_[CHUNK:] + pu_
                o = o * _rms_scale(o) * gnw
                oa_ref[bi, rows, hd * GDN_DV:(hd + 1) * GDN_DV] = (o * sz_ref[slot_r, j]).astype(BF16)

        return [stage_ws, stage_update]

    def pair_step(pr, carry):
        cs = [GDN_GROUP * pr + e for e in range(GDN_GROUP)]

        pieces = _Interleaver([piece for c in cs for piece in recurrence(c)],
                              period=max(1, 4 // GDN_GROUP), burst=max(1, GDN_GROUP // 4))

        items = [(e, bi, hd) for e in range(GDN_GROUP) for bi, hd in chains]
        rows = [pl.ds(pl.multiple_of(c * CHUNK, CHUNK), CHUNK) for c in cs]

        gcol = [[_exact_dot_01(tri_lo, gbc_ref[bi, c], ones_on_left=True) for bi in range(nb)] for c in cs]
        grow = [[_exact_dot_01(gbr_ref[bi, c], tri_up, ones_on_left=False) for bi in range(nb)] for c in cs]

        def load(col0):
            return [a4_ref[bi, rows[e], col0 + hd * GDN_DK:col0 + (hd + 1) * GDN_DK] for e, bi, hd in items]

        q, k, v = load(0), load(KEY_A), load(2 * KEY_A)
        g_c = [gcol[e][bi][:, GDN_HEADS + hd:GDN_HEADS + hd + 1] for e, bi, hd in items]
        g_r = [grow[e][bi][GDN_HEADS + hd:GDN_HEADS + hd + 1, :] for e, bi, hd in items]
        beta = [gbc_ref[bi, cs[e]][:, hd:hd + 1] for e, bi, hd in items]
        g_last = [x[CHUNK - 1:CHUNK, :] for x in g_c]
        dec = _each(lambda gc, gr: jnp.exp(jnp.where(incl, gc - gr, 0.0)), g_c, g_r)
        gam = _each(jnp.exp, g_c)

        kq = _each(lambda k_, q_: _dot_nt(jnp.concatenate([k_, q_], axis=0), k_), k, q)
        pieces()
        a_mat = _each(lambda b_, x, d_: jnp.where(strict, b_ * x[:CHUNK] * d_, 0.0), beta, kq, dec)
        p_mat = _each(lambda x, d_: jnp.where(incl, x[CHUNK:] * d_, 0.0), kq, dec)
        inv = _unit_lower_inverse(a_mat, ii, jj, pieces)
        rhs = _each(lambda b_, g_, k_, v_: jnp.concatenate([(b_ * g_) * k_, b_ * v_], axis=1),
                    beta, gam, k, v)
        sol = _each(_dot, inv, rhs)
        pieces.flush()

        z = load(CONV_A)
        for n, (e, bi, hd) in enumerate(items):
            j = cs[e] * n_ch + bi * GDN_HEADS + hd
            wq_ref[slot_w, j] = jnp.concatenate([sol[n][:, :GDN_DK], q[n] * gam[n]], axis=0).astype(BF16)
            uv_ref[slot_w, j] = sol[n][:, GDN_DK:]
            pm_ref[slot_w, j] = p_mat[n].astype(BF16)
            kd_ref[slot_w, j] = (k[n] * jnp.exp(g_last[n] - g_c[n])).astype(BF16)
            sz_ref[slot_w, j] = z[n] * _sigmoid(z[n])
            gl_ref[slot_w, j] = jnp.broadcast_to(g_last[n], gl_ref.shape[2:])
        return carry

    lax.fori_loop(0, nc // GDN_GROUP, pair_step, 0)


def _gated_delta(a4, gbc, gbr, gnw):
    b, t, _ = a4.shape
    nc = GDN_CHUNKS
    rows = nc * CHUNK
    n_tiles = t // rows
    per_tile = nc * b * GDN_HEADS

    def tile_in(i):
        return jnp.minimum(i, n_tiles - 1)

    def tile_out(i):
        return jnp.maximum(i - 1, 0)

    return pl.pallas_call(
        _gdn_kernel,
        grid=(n_tiles + 1,),
        in_specs=[
            pl.BlockSpec((b, rows, CONV_A + VAL_A), lambda i: (0, tile_in(i), 0)),
            pl.BlockSpec((b, nc, CHUNK, 2 * GDN_HEADS), lambda i: (0, tile_in(i), 0, 0)),
            pl.BlockSpec((b, nc, 2 * GDN_HEADS, CHUNK), lambda i: (0, tile_in(i), 0, 0)),
            _const_spec(gnw.shape),
        ],
        out_specs=pl.BlockSpec((b, rows, VAL_A), lambda i: (0, tile_out(i), 0)),
        out_shape=jax.ShapeDtypeStruct((b, t, VAL_A), BF16),
        scratch_shapes=[
            pltpu.VMEM((b * GDN_HEADS, GDN_DK, GDN_DV), F32),
            pltpu.VMEM((2, per_tile, 2 * CHUNK, GDN_DK), BF16),
            pltpu.VMEM((2, per_tile, CHUNK, GDN_DV), F32),
            pltpu.VMEM((2, per_tile, CHUNK, CHUNK), BF16),
            pltpu.VMEM((2, per_tile, CHUNK, GDN_DK), BF16),
            pltpu.VMEM((2, per_tile, CHUNK, GDN_DV), F32),
            pltpu.VMEM((2, per_tile, V7X_SUBLANES, V7X_LANES), F32),
        ],
        compiler_params=_params(1),
        name="gated_delta",
    )(a4, gbc, gbr, gnw)


def _attention_prepare(k_ref, v_ref, rel_ref, kbuf_ref, vbuf_ref, bias_ref):
    tq = k_ref.shape[1]
    i = pl.program_id(1)

    @pl.when((i == 0) & (pl.program_id(0) == 0))
    def _():
        for hd in range(ATT_HEADS):
            rel = jnp.broadcast_to(rel_ref[hd:hd + 1, :] * LOG2_E, (CHUNK, BIAS_ROLL))
            rolled = pltpu.roll(rel, 0, 1, stride=1, stride_axis=0)
            bias_ref[hd // 2, (hd % 2) * CHUNK:(hd % 2 + 1) * CHUNK, :] = rolled[:, :BAND]

    zeros = jnp.zeros((LEAD, WIDTH_B), BF16)
    kbuf_ref[0:LEAD, :] = jnp.where(i == 0, zeros, kbuf_ref[tq:tq + LEAD, :])
    vbuf_ref[0:LEAD, :] = jnp.where(i == 0, zeros, vbuf_ref[tq:tq + LEAD, :])
    kbuf_ref[LEAD:, :] = k_ref[0]
    vbuf_ref[LEAD:, :] = v_ref[0]


def _attention_chunk(c, q_ref, kbuf_ref, vbuf_ref, bias_ref, ob_ref):
    tq = q_ref.shape[1]
    i = pl.program_id(1)
    rows = slice(c * CHUNK, (c + 1) * CHUNK)
    band = slice(c * CHUNK, c * CHUNK + BAND)
    pairs = list(range(ATT_HEADS // 2))
    lanes = [slice(pr * 2 * ATT_DH, (pr + 1) * 2 * ATT_DH) for pr in pairs]
    first = lax.broadcasted_iota(jnp.int32, (CHUNK, 2 * ATT_DH), 1) < ATT_DH
    st = {}

    def scores():
        key = lax.broadcasted_iota(jnp.int32, (1, BAND), 1)
        valid = (i * tq - LEAD + c * CHUNK + key) >= 0
        zero_q = jnp.zeros((CHUNK, 2 * ATT_DH), BF16)

        def stacked_q(ln):
            q = q_ref[0, rows, ln]
            return jnp.concatenate([jnp.where(first, q, zero_q), jnp.where(first, zero_q, q)], axis=0)

        s = [_dot_nt(stacked_q(ln), kbuf_ref[band, ln]) for ln in lanes]
        s = [jnp.where(valid, x + bias_ref[pr], -jnp.inf) for x, pr in zip(s, pairs)]
        p = [jnp.exp2(x - jnp.max(x, axis=-1, keepdims=True)) for x in s]
        st["denom"] = [jnp.sum(x, axis=-1, keepdims=True) for x in p]
        st["p"] = [x.astype(BF16) for x in p]

    def values():
        pv = [_dot(x, vbuf_ref[band, ln]) for x, ln in zip(st["p"], lanes)]
        for x, dn, ln in zip(pv, st["denom"], lanes):
            x = x / dn
            ob_ref[rows, ln] = jnp.where(first, x[:CHUNK], x[CHUNK:]).astype(BF16)

    return [scores, values]


def _ffn_kernel(final_norm, x_ref, h_ref, oa_ref, q_ref, k_ref, v_ref, rel_ref, wg_ref, wa_ref,
                wb_ref, wo_ref, nfw_ref, wup_ref, cfw_ref, cfb_ref, wdn_ref, nlw_ref, out_ref,
                ubuf_ref, kbuf_ref, vbuf_ref, bias_ref, ob_ref):
    tm = x_ref.shape[1]
    d = x_ref.shape[2]
    dff = wdn_ref.shape[0]
    halo = V7X_SUBLANES
    x = x_ref[0]

    _attention_prepare(k_ref, v_ref, rel_ref, kbuf_ref, vbuf_ref, bias_ref)
    ubuf_ref[0:halo, :] = jnp.where(pl.program_id(1) == 0, 0.0, ubuf_ref[0:halo, :])
    attention = _Interleaver(
        [piece for c in range(tm // CHUNK)
         for piece in _attention_chunk(c, q_ref, kbuf_ref, vbuf_ref, bias_ref, ob_ref)], period=1)

    h = h_ref[0]
    gates = []
    for blk in range(2 * d // FFN_COLS):
        if blk % 2 == 0:
            attention()
        gates.append(jnp.dot(h, wg_ref[:, blk * FFN_COLS:(blk + 1) * FFN_COLS], preferred_element_type=F32))
        if blk % 2 == 1:
            attention()
    ya = jnp.dot(oa_ref[0], wa_ref[...], preferred_element_type=F32)
    attention.flush()
    gates = jnp.concatenate(gates, axis=1)
    yb = jnp.dot(ob_ref[...], wb_ref[...], preferred_element_type=F32)
    mix = _sigmoid(gates[:, :d]) * ya + _sigmoid(gates[:, d:]) * yb
    x1 = x + _dot(mix, wo_ref[...])

    h2 = (x1 * _rms_scale(x1) * nfw_ref[...]).astype(BF16)

    def gate_up_cols(blk):
        return [slice(half * dff + blk * FFN_COLS, half * dff + (blk + 1) * FFN_COLS) for half in range(2)]

    def project_up(blk):
        for cols in gate_up_cols(blk):
            ubuf_ref[halo:, cols] = jnp.dot(h2, wup_ref[:, cols], preferred_element_type=F32)

    def conv_block(cols):
        acc = cfb_ref[:, cols] + cfw_ref[FFN_CONV - 1:FFN_CONV, cols] * ubuf_ref[halo:halo + tm, cols]
        for tap in range(FFN_CONV - 1):
            off = halo - (FFN_CONV - 1) + tap
            acc = acc + cfw_ref[tap:tap + 1, cols] * ubuf_ref[off:off + tm, cols]
        ubuf_ref[0:halo, cols] = ubuf_ref[tm:tm + halo, cols]
        return acc

    n_blk = dff // FFN_COLS
    for blk in range(n_blk):
        project_up(blk)
    x2 = x1
    for blk in range(n_blk):
        gate, up = [conv_block(cols) for cols in gate_up_cols(blk)]
        act = (gate * _sigmoid(gate) * up).astype(BF16)
        x2 = x2 + jnp.dot(act, wdn_ref[blk * FFN_COLS:(blk + 1) * FFN_COLS, :], preferred_element_type=F32)
    if final_norm:
        x2 = x2 * _rms_scale(x2) * nlw_ref[...]
    out_ref[0] = x2


def _merge_ffn(x, h, oa, qkvb, rel, wg, wa, wb, wo, nfw, wup, cfw, cfb, wdn, nlw, final_norm):
    b, t, d = x.shape
    tm = FFN_ROWS
    dff = wdn.shape[0]
    consts = (rel, wg, wa, wb, wo, nfw, wup, cfw, cfb, wdn, nlw)
    return pl.pallas_call(
        functools.partial(_ffn_kernel, final_norm),
        grid=(b, t // tm),
        in_specs=[
            pl.BlockSpec((1, tm, d), lambda bi, i: (bi, i, 0)),
            pl.BlockSpec((1, tm, d), lambda bi, i: (bi, i, 0)),
            pl.BlockSpec((1, tm, VAL_A), lambda bi, i: (bi, i, 0)),
            pl.BlockSpec((1, tm, WIDTH_B), lambda bi, i: (bi, i, 0)),
            pl.BlockSpec((1, tm, WIDTH_B), lambda bi, i: (bi, i, 1)),
            pl.BlockSpec((1, tm, WIDTH_B), lambda bi, i: (bi, i, 2)),
        ] + [_const_spec(c.shape) for c in consts],
        out_specs=pl.BlockSpec((1, tm, d), lambda bi, i: (bi, i, 0)),
        out_shape=jax.ShapeDtypeStruct((b, t, d), F32),
        scratch_shapes=[
            pltpu.VMEM((tm + V7X_SUBLANES, 2 * dff), F32),
            pltpu.VMEM((LEAD + tm, WIDTH_B), BF16),
            pltpu.VMEM((LEAD + tm, WIDTH_B), BF16),
            pltpu.VMEM((ATT_HEADS // 2, 2 * CHUNK, BAND), F32),
            pltpu.VMEM((tm, WIDTH_B), BF16),
        ],
        compiler_params=_params(2),
        name="merge_ffn",
    )(x, h, oa, qkvb, qkvb, qkvb, *consts)


def _rel_bias_row(rel_table):
    n = jnp.arange(BIAS_ROLL)
    dist = jnp.where(n < BAND, jnp.clip(LEAD - n, -REL_CLIP, REL_CLIP), REL_CLIP)
    return rel_table.astype(F32)[:, dist + REL_CLIP]


def kernel(x, norm_mix_w, w_in, conv_qkv_w, a_log, dt_bias, gdn_norm_w, w_branch_a, w_branch_b,
           rel_bias, w_out, norm_ffn_w, w_up, conv_ffn_w, conv_ffn_b, w_down, norm_final_w):
    depth = w_in.shape[0]
    o_z = CONV_A + VAL_A
    o_bg = o_z + 2 * GDN_HEADS
    o_b = o_bg + 3 * WIDTH_B
    zeros_h = jnp.zeros((GDN_HEADS,), F32)
    for l in range(depth):
        w_l = w_in[l]

        def w_cols(lo, hi, w_l=w_l):
            return w_l[:, lo:hi].astype(BF16)

        wbg = w_cols(o_z, o_bg)
        hpr = jnp.stack([jnp.concatenate([zeros_h, a_log[l].astype(F32)]),
                         jnp.concatenate([zeros_h, dt_bias[l].astype(F32)])])
        a4, qkvb, gbc, gbr, h = _project(
            x, norm_mix_w[l][None, :], w_cols(0, o_z), w_cols(o_bg, o_b), wbg, wbg.T,
            conv_qkv_w[l], hpr, hpr.T)
        oa = _gated_delta(a4, gbc, gbr, gdn_norm_w[l][None, :])
        x = _merge_ffn(
            x, h, oa, qkvb, _rel_bias_row(rel_bias[l]), w_cols(o_b, w_l.shape[1]),
            w_branch_a[l].astype(BF16),
            w_branch_b[l].astype(BF16), w_out[l].astype(BF16), norm_ffn_w[l][None, :],
            w_up[l].astype(BF16), conv_ffn_w[l], conv_ffn_b[l][None, :], w_down[l].astype(BF16),
            norm_final_w[None, :], l == depth - 1)
    return x
```

```python
import functools

import jax
import jax.numpy as jnp
from jax import lax
from jax.experimental import pallas as pl
from jax.experimental.pallas import tpu as pltpu

CHUNK = 64
EPS = 1e-6
GDN_HEADS = 4
GDN_DK = 128
GDN_DV = 128
GDN_CONV = 4
ATT_HEADS = 8
ATT_DH = 64
ATT_BAND = 9
REL_CLIP = 128
FFN_CONV = 3

KEY_A = GDN_HEADS * GDN_DK
VAL_A = GDN_HEADS * GDN_DV
WIDTH_B = ATT_HEADS * ATT_DH
CONV_A = 2 * KEY_A + VAL_A
LEAD = (ATT_BAND - 1) * CHUNK
BAND = ATT_BAND * CHUNK

V7X_SUBLANES = 8
V7X_LANES = 128
V7X_VMEM_LIMIT_BYTES = 56 * 1024 * 1024

BIAS_ROLL = -(-(BAND + CHUNK - 1) // V7X_LANES) * V7X_LANES

PROJ_ROWS = 1024
PROJ_COLS = 256
GDN_CHUNKS = 8
GDN_GROUP = 4
FFN_ROWS = 256
FFN_COLS = 256

F32 = jnp.float32
BF16 = jnp.bfloat16
LOG2_E = 1.4426950408889634
NEG_LOG2_E = -LOG2_E


def _dot(a, b):
    return jnp.dot(a.astype(BF16), b.astype(BF16), preferred_element_type=F32)


def _dot_nt(a, b):
    return lax.dot_general(a.astype(BF16), b.astype(BF16), (((1,), (1,)), ((), ())),
                           preferred_element_type=F32)


def _dot_tn(a, b):
    return lax.dot_general(a.astype(BF16), b.astype(BF16), (((0,), (0,)), ((), ())),
                           preferred_element_type=F32)


def _sigmoid(x):
    return 1.0 / (1.0 + jnp.exp2(x * NEG_LOG2_E))


def _softplus(x):
    return jnp.maximum(x, 0.0) + jnp.log1p(jnp.exp(-jnp.abs(x)))


def _rms_scale(x):
    return lax.rsqrt(jnp.mean(x * x, axis=-1, keepdims=True) + EPS)


def _each(fn, *lists):
    return [fn(*args) for args in zip(*lists)]


class _Interleaver:
    def __init__(self, pieces, period, burst=1):
        self.pieces, self.period, self.burst, self.calls = list(pieces), period, burst, 0

    def __call__(self):
        self.calls += 1
        if self.calls % self.period == 0:
            for _ in range(min(self.burst, len(self.pieces))):
                self.pieces.pop(0)()

    def flush(self):
        while self.pieces:
            self.pieces.pop(0)()


def _const_spec(shape):
    zeros = (0,) * len(shape)
    return pl.BlockSpec(shape, lambda *_: zeros, pipeline_mode=pl.Buffered(1))


def _params(n_axes):
    return pltpu.CompilerParams(dimension_semantics=("arbitrary",) * n_axes,
                                vmem_limit_bytes=V7X_VMEM_LIMIT_BYTES)


def _proj_kernel(x_ref, nw_ref, wa_ref, wb_ref, wbgt_ref, cw_ref, hpc_ref,
                 a4_ref, qkvb_ref, gbc_ref, gbr_ref, h_ref, cbuf_ref):
    tm = x_ref.shape[1]
    halo = V7X_SUBLANES
    x = x_ref[0]
    h = (x * _rms_scale(x) * nw_ref[...]).astype(BF16)
    h_ref[0] = h

    pr = lax.dot_general(wbgt_ref[...], h, (((1,), (1,)), ((), ())),
                         preferred_element_type=F32)
    row = lax.broadcasted_iota(jnp.int32, pr.shape, 0)
    gr = -jnp.exp(hpc_ref[:, 0:1]) * _softplus(pr + hpc_ref[:, 1:2])
    valr = jnp.where(row < GDN_HEADS, _sigmoid(pr), gr)
    for c in range(tm // CHUNK):
        gbr_ref[0, c] = valr[:, c * CHUNK:(c + 1) * CHUNK]
    gbc_ref[0] = valr.T.reshape(tm // CHUNK, CHUNK, 2 * GDN_HEADS)

    cbuf_ref[0:halo, :] = jnp.where(pl.program_id(1) == 0, 0.0, cbuf_ref[0:halo, :])

    def project_a(blk):
        cols = slice(blk * PROJ_COLS, (blk + 1) * PROJ_COLS)
        cbuf_ref[halo:, cols] = jnp.dot(h, wa_ref[:, cols], preferred_element_type=F32)

    def project_z(blk):
        cols = slice(CONV_A + blk * PROJ_COLS, CONV_A + (blk + 1) * PROJ_COLS)
        a4_ref[0, :, cols] = jnp.dot(h, wa_ref[:, cols], preferred_element_type=F32)

    def project_b(blk):
        cols = slice(blk * PROJ_COLS, (blk + 1) * PROJ_COLS)
        pb = jnp.dot(h, wb_ref[:, cols], preferred_element_type=F32)
        if (blk + 1) * PROJ_COLS <= WIDTH_B:
            pb = pb * (ATT_DH ** -0.5 * LOG2_E)
        qkvb_ref[0, :, cols] = pb.astype(BF16)

    def conv_head(blk):
        cols = slice(blk * V7X_LANES, (blk + 1) * V7X_LANES)
        conv = cw_ref[GDN_CONV - 1:GDN_CONV, cols] * cbuf_ref[halo:halo + tm, cols]
        for tap in range(GDN_CONV - 1):
            off = halo - (GDN_CONV - 1) + tap
            conv = conv + cw_ref[tap:tap + 1, cols] * cbuf_ref[off:off + tm, cols]
        cbuf_ref[0:halo, cols] = cbuf_ref[tm:tm + halo, cols]
        act = conv * _sigmoid(conv)
        if blk < 2 * GDN_HEADS:
            inv_norm = lax.rsqrt(jnp.sum(act * act, axis=-1, keepdims=True) + EPS)
            act = act * (inv_norm * (GDN_DK ** -0.5) if blk < GDN_HEADS else inv_norm)
        a4_ref[0, :, cols] = act

    n_a = CONV_A // PROJ_COLS
    heads_per_blk = PROJ_COLS // V7X_LANES
    project_a(0)
    for blk in range(n_a):
        if blk + 1 < n_a:
            project_a(blk + 1)
        project_b(blk)
        if blk < VAL_A // PROJ_COLS:
            project_z(blk)
        for sub in range(heads_per_blk):
            conv_head(blk * heads_per_blk + sub)


def _project(x, nw, wa, wb, wbgt, cw, hpc):
    b, t, d = x.shape
    tm = PROJ_ROWS
    nc = tm // CHUNK
    return pl.pallas_call(
        _proj_kernel,
        grid=(b, t // tm),
        in_specs=[
            pl.BlockSpec((1, tm, d), lambda bi, i: (bi, i, 0)),
            _const_spec(nw.shape), _const_spec(wa.shape), _const_spec(wb.shape),
            _const_spec(wbgt.shape), _const_spec(cw.shape), _const_spec(hpc.shape),
        ],
        out_specs=[
            pl.BlockSpec((1, tm, CONV_A + VAL_A), lambda bi, i: (bi, i, 0)),
            pl.BlockSpec((1, tm, 3 * WIDTH_B), lambda bi, i: (bi, i, 0)),
            pl.BlockSpec((1, nc, CHUNK, 2 * GDN_HEADS), lambda bi, i: (bi, i, 0, 0)),
            pl.BlockSpec((1, nc, 2 * GDN_HEADS, CHUNK), lambda bi, i: (bi, i, 0, 0)),
            pl.BlockSpec((1, tm, d), lambda bi, i: (bi, i, 0)),
        ],
        out_shape=[
            jax.ShapeDtypeStruct((b, t, CONV_A + VAL_A), F32),
            jax.ShapeDtypeStruct((b, t, 3 * WIDTH_B), BF16),
            jax.ShapeDtypeStruct((b, t // CHUNK, CHUNK, 2 * GDN_HEADS), F32),
            jax.ShapeDtypeStruct((b, t // CHUNK, 2 * GDN_HEADS, CHUNK), F32),
            jax.ShapeDtypeStruct((b, t, d), BF16),
        ],
        scratch_shapes=[pltpu.VMEM((tm + V7X_SUBLANES, CONV_A), F32)],
        compiler_params=_params(2),
        name="in_proj",
    )(x, nw, wa, wb, wbgt, cw, hpc)


def _unit_lower_inverse(a, ii, jj, between):
    same16 = (ii // 16) == (jj // 16)
    same32 = (ii // 32) == (jj // 32)
    eye = jnp.where(ii == jj, 1.0, 0.0).astype(F32)
    ad = _each(lambda x: jnp.where(same16, x, 0.0), a)
    inv = _each(lambda x: eye - x, ad)
    power = _each(_dot, ad, ad)
    between()
    for level in range(3):
        inv = _each(lambda i, pi: i + pi, inv, _each(_dot, power, inv))
        if level < 2:
            power = _each(_dot, power, power)
        between()
    for off in (_each(lambda x: jnp.where(same32 & jnp.logical_not(same16), x, 0.0), a),
                _each(lambda x: jnp.where(same32, 0.0, x), a)):
        t = _each(_dot, off, inv)
        between()
        inv = _each(lambda i, t_: i - t_, inv, _each(_dot, inv, t))
        between()
    return inv


def _exact_dot_01(a, b, ones_on_left):
    x = b if ones_on_left else a
    mask = (a if ones_on_left else b).astype(BF16)
    acc = None
    for _ in range(3):
        piece = x.astype(BF16)
        x = x - piece.astype(F32)
        term = (jnp.dot(mask, piece, preferred_element_type=F32) if ones_on_left
                else jnp.dot(piece, mask, preferred_element_type=F32))
        acc = term if acc is None else acc + term
    return acc


def _gdn_kernel(a4_ref, gbc_ref, gbr_ref, gnw_ref, oa_ref,
                s_ref, wq_ref, uv_ref, pm_ref, kd_ref, sz_ref, gl_ref):
    nb = a4_ref.shape[0]
    nc = gbc_ref.shape[1]
    step = pl.program_id(0)
    slot_w = step % 2
    slot_r = 1 - slot_w
    chains = [(bi, hd) for bi in range(nb) for hd in range(GDN_HEADS)]
    n_ch = len(chains)

    @pl.when(step == 0)
    def _():
        s_ref[...] = jnp.zeros(s_ref.shape, F32)
        for ref in (wq_ref, uv_ref, pm_ref, kd_ref, sz_ref, gl_ref):
            ref[1] = jnp.zeros(ref.shape[1:], ref.dtype)

    ii = lax.broadcasted_iota(jnp.int32, (CHUNK, CHUNK), 0)
    jj = lax.broadcasted_iota(jnp.int32, (CHUNK, CHUNK), 1)
    incl = ii >= jj
    strict = ii > jj
    tri_lo = jnp.where(incl, 1.0, 0.0).astype(F32)
    tri_up = jnp.where(ii <= jj, 1.0, 0.0).astype(F32)
    gnw = gnw_ref[...]

    def recurrence(c):
        rows = pl.ds(pl.multiple_of(c * CHUNK, CHUNK), CHUNK)
        at = [c * n_ch + ch for ch in range(n_ch)]
        st = {}

        def stage_ws():
            st["s"] = [s_ref[ch] for ch in range(n_ch)]
            st["ws"] = [_dot(wq_ref[slot_r, j], s_) for j, s_ in zip(at, st["s"])]

        def stage_update():
            u = [uv_ref[slot_r, j] - ws_[:CHUNK] for j, ws_ in zip(at, st["ws"])]
            pu = [_dot(pm_ref[slot_r, j], u_) for j, u_ in zip(at, u)]
            ku = [_dot_tn(kd_ref[slot_r, j], u_) for j, u_ in zip(at, u)]
            for ch, (j, s_, ku_) in enumerate(zip(at, st["s"], ku)):
                s_ref[ch] = jnp.exp(gl_ref[slot_r, j][0:1, :]) * s_ + ku_
            for (bi, hd), j, ws_, pu_ in zip(chains, at, st["ws"], pu):
                o = ws_[CHUNK:] + pu_
                o = o * _rms_scale(o) * gnw
                oa_ref[bi, rows, hd * GDN_DV:(hd + 1) * GDN_DV] = (o * sz_ref[slot_r, j]).astype(BF16)

        return [stage_ws, stage_update]

    def pair_step(pr, carry):
        cs = [GDN_GROUP * pr + e for e in range(GDN_GROUP)]

        pieces = _Interleaver([piece for c in cs for piece in recurrence(c)],
                              period=max(1, 4 // GDN_GROUP), burst=max(1, GDN_GROUP // 4))

        items = [(e, bi, hd) for e in range(GDN_GROUP) for bi, hd in chains]
        rows = [pl.ds(pl.multiple_of(c * CHUNK, CHUNK), CHUNK) for c in cs]

        gcol = [[_exact_dot_01(tri_lo, gbc_ref[bi, c], ones_on_left=True) for bi in range(nb)] for c in cs]
        grow = [[_exact_dot_01(gbr_ref[bi, c], tri_up, ones_on_left=False) for bi in range(nb)] for c in cs]

        def load(col0):
            return [a4_ref[bi, rows[e], col0 + hd * GDN_DK:col0 + (hd + 1) * GDN_DK] for e, bi, hd in items]

        q, k, v = load(0), load(KEY_A), load(2 * KEY_A)
        g_c = [gcol[e][bi][:, GDN_HEADS + hd:GDN_HEADS + hd + 1] for e, bi, hd in items]
        g_r = [grow[e][bi][GDN_HEADS + hd:GDN_HEADS + hd + 1, :] for e, bi, hd in items]
        beta = [gbc_ref[bi, cs[e]][:, hd:hd + 1] for e, bi, hd in items]
        g_last = [x[CHUNK - 1:CHUNK, :] for x in g_c]
        dec = _each(lambda gc, gr: jnp.exp(jnp.where(incl, gc - gr, 0.0)), g_c, g_r)
        gam = _each(jnp.exp, g_c)

        kq = _each(lambda k_, q_: _dot_nt(jnp.concatenate([k_, q_], axis=0), k_), k, q)
        pieces()
        a_mat = _each(lambda b_, x, d_: jnp.where(strict, b_ * x[:CHUNK] * d_, 0.0), beta, kq, dec)
        p_mat = _each(lambda x, d_: jnp.where(incl, x[CHUNK:] * d_, 0.0), kq, dec)
        inv = _unit_lower_inverse(a_mat, ii, jj, pieces)
        rhs = _each(lambda b_, g_, k_, v_: jnp.concatenate([(b_ * g_) * k_, b_ * v_], axis=1),
                    beta, gam, k, v)
        sol = _each(_dot, inv, rhs)
        pieces.flush()

        z = load(CONV_A)
        for n, (e, bi, hd) in enumerate(items):
            j = cs[e] * n_ch + bi * GDN_HEADS + hd
            wq_ref[slot_w, j] = jnp.concatenate([sol[n][:, :GDN_DK], q[n] * gam[n]], axis=0).astype(BF16)
            uv_ref[slot_w, j] = sol[n][:, GDN_DK:]
            pm_ref[slot_w, j] = p_mat[n].astype(BF16)
            kd_ref[slot_w, j] = (k[n] * jnp.exp(g_last[n] - g_c[n])).astype(BF16)
            sz_ref[slot_w, j] = z[n] * _sigmoid(z[n])
            gl_ref[slot_w, j] = jnp.broadcast_to(g_last[n], gl_ref.shape[2:])
        return carry

    lax.fori_loop(0, nc // GDN_GROUP, pair_step, 0)


def _gated_delta(a4, gbc, gbr, gnw):
    b, t, _ = a4.shape
    nc = GDN_CHUNKS
    rows = nc * CHUNK
    n_tiles = t // rows
    per_tile = nc * b * GDN_HEADS

    def tile_in(i):
        return jnp.minimum(i, n_tiles - 1)

    def tile_out(i):
        return jnp.maximum(i - 1, 0)

    return pl.pallas_call(
        _gdn_kernel,
        grid=(n_tiles + 1,),
        in_specs=[
            pl.BlockSpec((b, rows, CONV_A + VAL_A), lambda i: (0, tile_in(i), 0)),
            pl.BlockSpec((b, nc, CHUNK, 2 * GDN_HEADS), lambda i: (0, tile_in(i), 0, 0)),
            pl.BlockSpec((b, nc, 2 * GDN_HEADS, CHUNK), lambda i: (0, tile_in(i), 0, 0)),
            _const_spec(gnw.shape),
        ],
        out_specs=pl.BlockSpec((b, rows, VAL_A), lambda i: (0, tile_out(i), 0)),
        out_shape=jax.ShapeDtypeStruct((b, t, VAL_A), BF16),
        scratch_shapes=[
            pltpu.VMEM((b * GDN_HEADS, GDN_DK, GDN_DV), F32),
            pltpu.VMEM((2, per_tile, 2 * CHUNK, GDN_DK), BF16),
            pltpu.VMEM((2, per_tile, CHUNK, GDN_DV), F32),
            pltpu.VMEM((2, per_tile, CHUNK, CHUNK), BF16),
            pltpu.VMEM((2, per_tile, CHUNK, GDN_DK), BF16),
            pltpu.VMEM((2, per_tile, CHUNK, GDN_DV), F32),
            pltpu.VMEM((2, per_tile, V7X_SUBLANES, V7X_LANES), F32),
        ],
        compiler_params=_params(1),
        name="gated_delta",
    )(a4, gbc, gbr, gnw)


def _attention_prepare(k_ref, v_ref, rel_ref, kbuf_ref, vbuf_ref, bias_ref):
    tq = k_ref.shape[1]
    i = pl.program_id(1)

    @pl.when((i == 0) & (pl.program_id(0) == 0))
    def _():
        for hd in range(ATT_HEADS):
            rel = jnp.broadcast_to(rel_ref[hd:hd + 1, :] * LOG2_E, (CHUNK, BIAS_ROLL))
            rolled = pltpu.roll(rel, 0, 1, stride=1, stride_axis=0)
            bias_ref[hd // 2, (hd % 2) * CHUNK:(hd % 2 + 1) * CHUNK, :] = rolled[:, :BAND]

    zeros = jnp.zeros((LEAD, WIDTH_B), BF16)
    kbuf_ref[0:LEAD, :] = jnp.where(i == 0, zeros, kbuf_ref[tq:tq + LEAD, :])
    vbuf_ref[0:LEAD, :] = jnp.where(i == 0, zeros, vbuf_ref[tq:tq + LEAD, :])
    kbuf_ref[LEAD:, :] = k_ref[0]
    vbuf_ref[LEAD:, :] = v_ref[0]


def _attention_chunk(c, q_ref, kbuf_ref, vbuf_ref, bias_ref, ob_ref):
    tq = q_ref.shape[1]
    i = pl.program_id(1)
    rows = slice(c * CHUNK, (c + 1) * CHUNK)
    band = slice(c * CHUNK, c * CHUNK + BAND)
    pairs = list(range(ATT_HEADS // 2))
    lanes = [slice(pr * 2 * ATT_DH, (pr + 1) * 2 * ATT_DH) for pr in pairs]
    first = lax.broadcasted_iota(jnp.int32, (CHUNK, 2 * ATT_DH), 1) < ATT_DH
    st = {}

    def scores():
        key = lax.broadcasted_iota(jnp.int32, (1, BAND), 1)
        valid = (i * tq - LEAD + c * CHUNK + key) >= 0
        zero_q = jnp.zeros((CHUNK, 2 * ATT_DH), BF16)

        def stacked_q(ln):
            q = q_ref[0, rows, ln]
            return jnp.concatenate([jnp.where(first, q, zero_q), jnp.where(first, zero_q, q)], axis=0)

        s = [_dot_nt(stacked_q(ln), kbuf_ref[band, ln]) for ln in lanes]
        s = [jnp.where(valid, x + bias_ref[pr], -jnp.inf) for x, pr in zip(s, pairs)]
        p = [jnp.exp2(x - jnp.max(x, axis=-1, keepdims=True)) for x in s]
        st["denom"] = [jnp.sum(x, axis=-1, keepdims=True) for x in p]
        st["p"] = [x.astype(BF16) for x in p]

    def values():
        pv = [_dot(x, vbuf_ref[band, ln]) for x, ln in zip(st["p"], lanes)]
        for x, dn, ln in zip(pv, st["denom"], lanes):
            x = x / dn
            ob_ref[rows, ln] = jnp.where(first, x[:CHUNK], x[CHUNK:]).astype(BF16)

    return [scores, values]


def _ffn_kernel(final_norm, x_ref, h_ref, oa_ref, q_ref, k_ref, v_ref, rel_ref, wg_ref, wa_ref,
                wb_ref, wo_ref, nfw_ref, wup_ref, cfw_ref, cfb_ref, wdn_ref, nlw_ref, out_ref,
                ubuf_ref, kbuf_ref, vbuf_ref, bias_ref, ob_ref):
    tm = x_ref.shape[1]
    d = x_ref.shape[2]
    dff = wdn_ref.shape[0]
    halo = V7X_SUBLANES
    x = x_ref[0]

    _attention_prepare(k_ref, v_ref, rel_ref, kbuf_ref, vbuf_ref, bias_ref)
    ubuf_ref[0:halo, :] = jnp.where(pl.program_id(1) == 0, 0.0, ubuf_ref[0:halo, :])
    attention = _Interleaver(
        [piece for c in range(tm // CHUNK)
         for piece in _attention_chunk(c, q_ref, kbuf_ref, vbuf_ref, bias_ref, ob_ref)], period=1)

    h = h_ref[0]
    gates = []
    for blk in range(2 * d // FFN_COLS):
        if blk % 2 == 0:
            attention()
        gates.append(jnp.dot(h, wg_ref[:, blk * FFN_COLS:(blk + 1) * FFN_COLS], preferred_element_type=F32))
        if blk % 2 == 1:
            attention()
    ya = jnp.dot(oa_ref[0], wa_ref[...], preferred_element_type=F32)
    attention.flush()
    gates = jnp.concatenate(gates, axis=1)
    yb = jnp.dot(ob_ref[...], wb_ref[...], preferred_element_type=F32)
    mix = _sigmoid(gates[:, :d]) * ya + _sigmoid(gates[:, d:]) * yb
    x1 = x + _dot(mix, wo_ref[...])

    h2 = (x1 * _rms_scale(x1) * nfw_ref[...]).astype(BF16)

    def gate_up_cols(blk):
        return [slice(half * dff + blk * FFN_COLS, half * dff + (blk + 1) * FFN_COLS) for half in range(2)]

    def project_up(blk):
        for cols in gate_up_cols(blk):
            ubuf_ref[halo:, cols] = jnp.dot(h2, wup_ref[:, cols], preferred_element_type=F32)

    def conv_block(cols):
        acc = cfb_ref[:, cols] + cfw_ref[FFN_CONV - 1:FFN_CONV, cols] * ubuf_ref[halo:halo + tm, cols]
        for tap in range(FFN_CONV - 1):
            off = halo - (FFN_CONV - 1) + tap
            acc = acc + cfw_ref[tap:tap + 1, cols] * ubuf_ref[off:off + tm, cols]
        ubuf_ref[0:halo, cols] = ubuf_ref[tm:tm + halo, cols]
        return acc

    n_blk = dff // FFN_COLS
    for blk in range(n_blk):
        project_up(blk)
    x2 = x1
    for blk in range(n_blk):
        gate, up = [conv_block(cols) for cols in gate_up_cols(blk)]
        act = (gate * _sigmoid(gate) * up).astype(BF16)
        x2 = x2 + jnp.dot(act, wdn_ref[blk * FFN_COLS:(blk + 1) * FFN_COLS, :], preferred_element_type=F32)
    if final_norm:
        x2 = x2 * _rms_scale(x2) * nlw_ref[...]
    out_ref[0] = x2


def _merge_ffn(x, h, oa, qkvb, rel, wg, wa, wb, wo, nfw, wup, cfw, cfb, wdn, nlw, final_norm):
    b, t, d = x.shape
    tm = FFN_ROWS
    dff = wdn.shape[0]
    consts = (rel, wg, wa, wb, wo, nfw, wup, cfw, cfb, wdn, nlw)
    return pl.pallas_call(
        functools.partial(_ffn_kernel, final_norm),
        grid=(b, t // tm),
        in_specs=[
            pl.BlockSpec((1, tm, d), lambda bi, i: (bi, i, 0)),
            pl.BlockSpec((1, tm, d), lambda bi, i: (bi, i, 0)),
            pl.BlockSpec((1, tm, VAL_A), lambda bi, i: (bi, i, 0)),
            pl.BlockSpec((1, tm, WIDTH_B), lambda bi, i: (bi, i, 0)),
            pl.BlockSpec((1, tm, WIDTH_B), lambda bi, i: (bi, i, 1)),
            pl.BlockSpec((1, tm, WIDTH_B), lambda bi, i: (bi, i, 2)),
        ] + [_const_spec(c.shape) for c in consts],
        out_specs=pl.BlockSpec((1, tm, d), lambda bi, i: (bi, i, 0)),
        out_shape=jax.ShapeDtypeStruct((b, t, d), F32),
        scratch_shapes=[
            pltpu.VMEM((tm + V7X_SUBLANES, 2 * dff), F32),
            pltpu.VMEM((LEAD + tm, WIDTH_B), BF16),
            pltpu.VMEM((LEAD + tm, WIDTH_B), BF16),
            pltpu.VMEM((ATT_HEADS // 2, 2 * CHUNK, BAND), F32),
            pltpu.VMEM((tm, WIDTH_B), BF16),
        ],
        compiler_params=_params(2),
        name="merge_ffn",
    )(x, h, oa, qkvb, qkvb, qkvb, *consts)


def _rel_bias_row(rel_table):
    n = jnp.arange(BIAS_ROLL)
    dist = jnp.where(n < BAND, jnp.clip(LEAD - n, -REL_CLIP, REL_CLIP), REL_CLIP)
    return rel_table.astype(F32)[:, dist + REL_CLIP]


def kernel(x, norm_mix_w, w_in, conv_qkv_w, a_log, dt_bias, gdn_norm_w, w_branch_a, w_branch_b,
           rel_bias, w_out, norm_ffn_w, w_up, conv_ffn_w, conv_ffn_b, w_down, norm_final_w):
    depth = w_in.shape[0]
    o_z = CONV_A + VAL_A
    o_bg = o_z + 2 * GDN_HEADS
    o_b = o_bg + 3 * WIDTH_B
    zeros_h = jnp.zeros((GDN_HEADS,), F32)
    for l in range(depth):
        w_l = w_in[l]

        def w_cols(lo, hi, w_l=w_l):
            return w_l[:, lo:hi].astype(BF16)

        head_params = jnp.stack([jnp.concatenate([zeros_h, a_log[l].astype(F32)]),
                                 jnp.concatenate([zeros_h, dt_bias[l].astype(F32)])], axis=1)
        a4, qkvb, gbc, gbr, h = _project(
            x, norm_mix_w[l][None, :], w_cols(0, o_z), w_cols(o_bg, o_b), w_cols(o_z, o_bg).T,
            conv_qkv_w[l], head_params)
        oa = _gated_delta(a4, gbc, gbr, gdn_norm_w[l][None, :])
        x = _merge_ffn(
            x, h, oa, qkvb, _rel_bias_row(rel_bias[l]), w_cols(o_b, w_l.shape[1]),
            w_branch_a[l].astype(BF16),
            w_branch_b[l].astype(BF16), w_out[l].astype(BF16), norm_ffn_w[l][None, :],
            w_up[l].astype(BF16), conv_ffn_w[l], conv_ffn_b[l][None, :], w_down[l].astype(BF16),
            norm_final_w[None, :], l == depth - 1)
    return x
```

```python
import functools

import jax
import jax.numpy as jnp
from jax import lax
from jax.experimental import pallas as pl
from jax.experimental.pallas import tpu as pltpu

CHUNK = 64
EPS = 1e-6
GDN_HEADS = 4
GDN_DK = 128
GDN_DV = 128
GDN_CONV = 4
ATT_HEADS = 8
ATT_DH = 64
ATT_BAND = 9
REL_CLIP = 128
FFN_CONV = 3

KEY_A = GDN_HEADS * GDN_DK
VAL_A = GDN_HEADS * GDN_DV
WIDTH_B = ATT_HEADS * ATT_DH
CONV_A = 2 * KEY_A + VAL_A
LEAD = (ATT_BAND - 1) * CHUNK
BAND = ATT_BAND * CHUNK

V7X_SUBLANES = 8
V7X_LANES = 128
V7X_VMEM_LIMIT_BYTES = 56 * 1024 * 1024

BIAS_ROLL = -(-(BAND + CHUNK - 1) // V7X_LANES) * V7X_LANES

PROJ_ROWS = 1024
PROJ_COLS = 256
GDN_CHUNKS = 8
GDN_GROUP = 4
FFN_ROWS = 256
FFN_COLS = 256

F32 = jnp.float32
BF16 = jnp.bfloat16
LOG2_E = 1.4426950408889634
NEG_LOG2_E = -LOG2_E


def _dot(a, b):
    return jnp.dot(a.astype(BF16), b.astype(BF16), preferred_element_type=F32)


def _dot_nt(a, b):
    return lax.dot_general(a.astype(BF16), b.astype(BF16), (((1,), (1,)), ((), ())),
                           preferred_element_type=F32)


def _dot_tn(a, b):
    return lax.dot_general(a.astype(BF16), b.astype(BF16), (((0,), (0,)), ((), ())),
                           preferred_element_type=F32)


def _sigmoid(x):
    return 1.0 / (1.0 + jnp.exp2(x * NEG_LOG2_E))


def _softplus(x):
    return jnp.maximum(x, 0.0) + jnp.log1p(jnp.exp(-jnp.abs(x)))


def _rms_scale(x):
    return lax.rsqrt(jnp.mean(x * x, axis=-1, keepdims=True) + EPS)


def _each(fn, *lists):
    return [fn(*args) for args in zip(*lists)]


class _Interleaver:
    def __init__(self, pieces, period, burst=1):
        self.pieces, self.period, self.burst, self.calls = list(pieces), period, burst, 0

    def __call__(self):
        self.calls += 1
        if self.calls % self.period == 0:
            for _ in range(min(self.burst, len(self.pieces))):
                self.pieces.pop(0)()

    def flush(self):
        while self.pieces:
            self.pieces.pop(0)()


def _const_spec(shape):
    zeros = (0,) * len(shape)
    return pl.BlockSpec(shape, lambda *_: zeros, pipeline_mode=pl.Buffered(1))


def _params(n_axes):
    return pltpu.CompilerParams(dimension_semantics=("arbitrary",) * n_axes,
                                vmem_limit_bytes=V7X_VMEM_LIMIT_BYTES)


def _proj_kernel(x_ref, nw_ref, wa_ref, wb_ref, wbgt_ref, cw_ref, hpc_ref,
                 a4_ref, qkvb_ref, gbc_ref, gbr_ref, h_ref, cbuf_ref, tri_ref):
    tm = x_ref.shape[1]
    halo = V7X_SUBLANES
    x = x_ref[0]
    h = (x * _rms_scale(x) * nw_ref[...]).astype(BF16)
    h_ref[0] = h

    pr = lax.dot_general(wbgt_ref[...], h, (((1,), (1,)), ((), ())),
                         preferred_element_type=F32)
    row = lax.broadcasted_iota(jnp.int32, pr.shape, 0)
    gr = -jnp.exp(hpc_ref[:, 0:1]) * _softplus(pr + hpc_ref[:, 1:2])
    valr = jnp.where(row < GDN_HEADS, _sigmoid(pr), gr)
    @pl.when((pl.program_id(0) == 0) & (pl.program_id(1) == 0))
    def _():
        src = lax.broadcasted_iota(jnp.int32, (tm, tm), 0)
        dst = lax.broadcasted_iota(jnp.int32, (tm, tm), 1)
        tri_ref[...] = jnp.where((src <= dst) & (src // CHUNK == dst // CHUNK), 1.0, 0.0).astype(BF16)

    valr = jnp.where(row < GDN_HEADS, valr, _exact_dot_01(valr, tri_ref[...], ones_on_left=False))
    for c in range(tm // CHUNK):
        gbr_ref[0, c] = valr[:, c * CHUNK:(c + 1) * CHUNK]
    gbc_ref[0] = valr.T.reshape(tm // CHUNK, CHUNK, 2 * GDN_HEADS)

    cbuf_ref[0:halo, :] = jnp.where(pl.program_id(1) == 0, 0.0, cbuf_ref[0:halo, :])

    def project_a(blk):
        cols = slice(blk * PROJ_COLS, (blk + 1) * PROJ_COLS)
        cbuf_ref[halo:, cols] = jnp.dot(h, wa_ref[:, cols], preferred_element_type=F32)

    def project_z(blk):
        cols = slice(CONV_A + blk * PROJ_COLS, CONV_A + (blk + 1) * PROJ_COLS)
        a4_ref[0, :, cols] = jnp.dot(h, wa_ref[:, cols], preferred_element_type=F32)

    def project_b(blk):
        cols = slice(blk * PROJ_COLS, (blk + 1) * PROJ_COLS)
        pb = jnp.dot(h, wb_ref[:, cols], preferred_element_type=F32)
        if (blk + 1) * PROJ_COLS <= WIDTH_B:
            pb = pb * (ATT_DH ** -0.5 * LOG2_E)
        qkvb_ref[0, :, cols] = pb.astype(BF16)

    def conv_head(blk):
        cols = slice(blk * V7X_LANES, (blk + 1) * V7X_LANES)
        conv = cw_ref[GDN_CONV - 1:GDN_CONV, cols] * cbuf_ref[halo:halo + tm, cols]
        for tap in range(GDN_CONV - 1):
            off = halo - (GDN_CONV - 1) + tap
            conv = conv + cw_ref[tap:tap + 1, cols] * cbuf_ref[off:off + tm, cols]
        cbuf_ref[0:halo, cols] = cbuf_ref[tm:tm + halo, cols]
        act = conv * _sigmoid(conv)
        if blk < 2 * GDN_HEADS:
            inv_norm = lax.rsqrt(jnp.sum(act * act, axis=-1, keepdims=True) + EPS)
            act = act * (inv_norm * (GDN_DK ** -0.5) if blk < GDN_HEADS else inv_norm)
        a4_ref[0, :, cols] = act

    n_a = CONV_A // PROJ_COLS
    heads_per_blk = PROJ_COLS // V7X_LANES
    project_a(0)
    for blk in range(n_a):
        if blk + 1 < n_a:
            project_a(blk + 1)
        project_b(blk)
        if blk < VAL_A // PROJ_COLS:
            project_z(blk)
        for sub in range(heads_per_blk):
            conv_head(blk * heads_per_blk + sub)


def _project(x, nw, wa, wb, wbgt, cw, hpc):
    b, t, d = x.shape
    tm = PROJ_ROWS
    nc = tm // CHUNK
    return pl.pallas_call(
        _proj_kernel,
        grid=(b, t // tm),
        in_specs=[
            pl.BlockSpec((1, tm, d), lambda bi, i: (bi, i, 0)),
            _const_spec(nw.shape), _const_spec(wa.shape), _const_spec(wb.shape),
            _const_spec(wbgt.shape), _const_spec(cw.shape), _const_spec(hpc.shape),
        ],
        out_specs=[
            pl.BlockSpec((1, tm, CONV_A + VAL_A), lambda bi, i: (bi, i, 0)),
            pl.BlockSpec((1, tm, 3 * WIDTH_B), lambda bi, i: (bi, i, 0)),
            pl.BlockSpec((1, nc, CHUNK, 2 * GDN_HEADS), lambda bi, i: (bi, i, 0, 0)),
            pl.BlockSpec((1, nc, 2 * GDN_HEADS, CHUNK), lambda bi, i: (bi, i, 0, 0)),
            pl.BlockSpec((1, tm, d), lambda bi, i: (bi, i, 0)),
        ],
        out_shape=[
            jax.ShapeDtypeStruct((b, t, CONV_A + VAL_A), F32),
            jax.ShapeDtypeStruct((b, t, 3 * WIDTH_B), BF16),
            jax.ShapeDtypeStruct((b, t // CHUNK, CHUNK, 2 * GDN_HEADS), F32),
            jax.ShapeDtypeStruct((b, t // CHUNK, 2 * GDN_HEADS, CHUNK), F32),
            jax.ShapeDtypeStruct((b, t, d), BF16),
        ],
        scratch_shapes=[pltpu.VMEM((tm + V7X_SUBLANES, CONV_A), F32),
                        pltpu.VMEM((tm, tm), BF16)],
        compiler_params=_params(2),
        name="in_proj",
    )(x, nw, wa, wb, wbgt, cw, hpc)


def _unit_lower_inverse(a, ii, jj, between):
    same16 = (ii // 16) == (jj // 16)
    same32 = (ii // 32) == (jj // 32)
    eye = jnp.where(ii == jj, 1.0, 0.0).astype(F32)
    ad = _each(lambda x: jnp.where(same16, x, 0.0), a)
    inv = _each(lambda x: eye - x, ad)
    power = _each(_dot, ad, ad)
    between()
    for level in range(3):
        inv = _each(lambda i, pi: i + pi, inv, _each(_dot, power, inv))
        if level < 2:
            power = _each(_dot, power, power)
        between()
    for off in (_each(lambda x: jnp.where(same32 & jnp.logical_not(same16), x, 0.0), a),
                _each(lambda x: jnp.where(same32, 0.0, x), a)):
        t = _each(_dot, off, inv)
        between()
        inv = _each(lambda i, t_: i - t_, inv, _each(_dot, inv, t))
        between()
    return inv


def _exact_dot_01(a, b, ones_on_left):
    x = b if ones_on_left else a
    mask = (a if ones_on_left else b).astype(BF16)
    acc = None
    for _ in range(3):
        piece = x.astype(BF16)
        x = x - piece.astype(F32)
        term = (jnp.dot(mask, piece, preferred_element_type=F32) if ones_on_left
                else jnp.dot(piece, mask, preferred_element_type=F32))
        acc = term if acc is None else acc + term
    return acc


def _gdn_kernel(a4_ref, gbc_ref, gbr_ref, gnw_ref, oa_ref,
                s_ref, wq_ref, uv_ref, pm_ref, kd_ref, sz_ref, gl_ref):
    nb = a4_ref.shape[0]
    nc = gbc_ref.shape[1]
    step = pl.program_id(0)
    slot_w = step % 2
    slot_r = 1 - slot_w
    chains = [(bi, hd) for bi in range(nb) for hd in range(GDN_HEADS)]
    n_ch = len(chains)

    @pl.when(step == 0)
    def _():
        s_ref[...] = jnp.zeros(s_ref.shape, F32)
        for ref in (wq_ref, uv_ref, pm_ref, kd_ref, sz_ref, gl_ref):
            ref[1] = jnp.zeros(ref.shape[1:], ref.dtype)

    ii = lax.broadcasted_iota(jnp.int32, (CHUNK, CHUNK), 0)
    jj = lax.broadcasted_iota(jnp.int32, (CHUNK, CHUNK), 1)
    incl = ii >= jj
    strict = ii > jj
    gnw = gnw_ref[...]

    def recurrence(c):
        rows = pl.ds(pl.multiple_of(c * CHUNK, CHUNK), CHUNK)
        at = [c * n_ch + ch for ch in range(n_ch)]
        st = {}

        def stage_ws():
            st["s"] = [s_ref[ch] for ch in range(n_ch)]
            st["ws"] = [_dot(wq_ref[slot_r, j], s_) for j, s_ in zip(at, st["s"])]

        def stage_update():
            u = [uv_ref[slot_r, j] - ws_[:CHUNK] for j, ws_ in zip(at, st["ws"])]
            pu = [_dot(pm_ref[slot_r, j], u_) for j, u_ in zip(at, u)]
            ku = [_dot_tn(kd_ref[slot_r, j], u_) for j, u_ in zip(at, u)]
            for ch, (j, s_, ku_) in enumerate(zip(at, st["s"], ku)):
                s_ref[ch] = jnp.exp(gl_ref[slot_r, j][0:1, :]) * s_ + ku_
            for (bi, hd), j, ws_, pu_ in zip(chains, at, st["ws"], pu):
                o = ws_[CHUNK:] + pu_
                o = o * _rms_scale(o) * gnw
                oa_ref[bi, rows, hd * GDN_DV:(hd + 1) * GDN_DV] = (o * sz_ref[slot_r, j]).astype(BF16)

        return [stage_ws, stage_update]

    def pair_step(pr, carry):
        cs = [GDN_GROUP * pr + e for e in range(GDN_GROUP)]

        pieces = _Interleaver([piece for c in cs for piece in recurrence(c)],
                              period=max(1, 4 // GDN_GROUP), burst=max(1, GDN_GROUP // 4))

        items = [(e, bi, hd) for e in range(GDN_GROUP) for bi, hd in chains]
        rows = [pl.ds(pl.multiple_of(c * CHUNK, CHUNK), CHUNK) for c in cs]

        gcol = [[gbc_ref[bi, c] for bi in range(nb)] for c in cs]
        grow = [[gbr_ref[bi, c] for bi in range(nb)] for c in cs]

        def load(col0):
            return [a4_ref[bi, rows[e], col0 + hd * GDN_DK:col0 + (hd + 1) * GDN_DK] for e, bi, hd in items]

        q, k, v = load(0), load(KEY_A), load(2 * KEY_A)
        g_c = [gcol[e][bi][:, GDN_HEADS + hd:GDN_HEADS + hd + 1] for e, bi, hd in items]
        g_r = [grow[e][bi][GDN_HEADS + hd:GDN_HEADS + hd + 1, :] for e, bi, hd in items]
        beta = [gbc_ref[bi, cs[e]][:, hd:hd + 1] for e, bi, hd in items]
        g_last = [x[CHUNK - 1:CHUNK, :] for x in g_c]
        dec = _each(lambda gc, gr: jnp.exp(jnp.where(incl, gc - gr, 0.0)), g_c, g_r)
        gam = _each(jnp.exp, g_c)

        kq = _each(lambda k_, q_: _dot_nt(jnp.concatenate([k_, q_], axis=0), k_), k, q)
        pieces()
        a_mat = _each(lambda b_, x, d_: jnp.where(strict, b_ * x[:CHUNK] * d_, 0.0), beta, kq, dec)
        p_mat = _each(lambda x, d_: jnp.where(incl, x[CHUNK:] * d_, 0.0), kq, dec)
        inv = _unit_lower_inverse(a_mat, ii, jj, pieces)
        rhs = _each(lambda b_, g_, k_, v_: jnp.concatenate([(b_ * g_) * k_, b_ * v_], axis=1),
                    beta, gam, k, v)
        sol = _each(_dot, inv, rhs)
        pieces.flush()

        z = load(CONV_A)
        for n, (e, bi, hd) in enumerate(items):
            j = cs[e] * n_ch + bi * GDN_HEADS + hd
            wq_ref[slot_w, j] = jnp.concatenate([sol[n][:, :GDN_DK], q[n] * gam[n]], axis=0).astype(BF16)
            uv_ref[slot_w, j] = sol[n][:, GDN_DK:]
            pm_ref[slot_w, j] = p_mat[n].astype(BF16)
            kd_ref[slot_w, j] = (k[n] * jnp.exp(g_last[n] - g_c[n])).astype(BF16)
            sz_ref[slot_w, j] = z[n] * _sigmoid(z[n])
            gl_ref[slot_w, j] = jnp.broadcast_to(g_last[n], gl_ref.shape[2:])
        return carry

    lax.fori_loop(0, nc // GDN_GROUP, pair_step, 0)


def _gated_delta(a4, gbc, gbr, gnw):
    b, t, _ = a4.shape
    nc = GDN_CHUNKS
    rows = nc * CHUNK
    n_tiles = t // rows
    per_tile = nc * b * GDN_HEADS

    def tile_in(i):
        return jnp.minimum(i, n_tiles - 1)

    def tile_out(i):
        return jnp.maximum(i - 1, 0)

    return pl.pallas_call(
        _gdn_kernel,
        grid=(n_tiles + 1,),
        in_specs=[
            pl.BlockSpec((b, rows, CONV_A + VAL_A), lambda i: (0, tile_in(i), 0)),
            pl.BlockSpec((b, nc, CHUNK, 2 * GDN_HEADS), lambda i: (0, tile_in(i), 0, 0)),
            pl.BlockSpec((b, nc, 2 * GDN_HEADS, CHUNK), lambda i: (0, tile_in(i), 0, 0)),
            _const_spec(gnw.shape),
        ],
        out_specs=pl.BlockSpec((b, rows, VAL_A), lambda i: (0, tile_out(i), 0)),
        out_shape=jax.ShapeDtypeStruct((b, t, VAL_A), BF16),
        scratch_shapes=[
            pltpu.VMEM((b * GDN_HEADS, GDN_DK, GDN_DV), F32),
            pltpu.VMEM((2, per_tile, 2 * CHUNK, GDN_DK), BF16),
            pltpu.VMEM((2, per_tile, CHUNK, GDN_DV), F32),
            pltpu.VMEM((2, per_tile, CHUNK, CHUNK), BF16),
            pltpu.VMEM((2, per_tile, CHUNK, GDN_DK), BF16),
            pltpu.VMEM((2, per_tile, CHUNK, GDN_DV), F32),
            pltpu.VMEM((2, per_tile, V7X_SUBLANES, V7X_LANES), F32),
        ],
        compiler_params=_params(1),
        name="gated_delta",
    )(a4, gbc, gbr, gnw)


def _attention_prepare(k_ref, v_ref, rel_ref, kbuf_ref, vbuf_ref, bias_ref):
    tq = k_ref.shape[1]
    i = pl.program_id(1)

    @pl.when((i == 0) & (pl.program_id(0) == 0))
    def _():
        for hd in range(ATT_HEADS):
            rel = jnp.broadcast_to(rel_ref[hd:hd + 1, :] * LOG2_E, (CHUNK, BIAS_ROLL))
            rolled = pltpu.roll(rel, 0, 1, stride=1, stride_axis=0)
            bias_ref[hd // 2, (hd % 2) * CHUNK:(hd % 2 + 1) * CHUNK, :] = rolled[:, :BAND]

    zeros = jnp.zeros((LEAD, WIDTH_B), BF16)
    kbuf_ref[0:LEAD, :] = jnp.where(i == 0, zeros, kbuf_ref[tq:tq + LEAD, :])
    vbuf_ref[0:LEAD, :] = jnp.where(i == 0, zeros, vbuf_ref[tq:tq + LEAD, :])
    kbuf_ref[LEAD:, :] = k_ref[0]
    vbuf_ref[LEAD:, :] = v_ref[0]


def _attention_chunk(c, q_ref, kbuf_ref, vbuf_ref, bias_ref, ob_ref):
    tq = q_ref.shape[1]
    i = pl.program_id(1)
    rows = slice(c * CHUNK, (c + 1) * CHUNK)
    band = slice(c * CHUNK, c * CHUNK + BAND)
    pairs = list(range(ATT_HEADS // 2))
    lanes = [slice(pr * 2 * ATT_DH, (pr + 1) * 2 * ATT_DH) for pr in pairs]
    first = lax.broadcasted_iota(jnp.int32, (CHUNK, 2 * ATT_DH), 1) < ATT_DH
    st = {}

    def scores():
        key = lax.broadcasted_iota(jnp.int32, (1, BAND), 1)
        valid = (i * tq - LEAD + c * CHUNK + key) >= 0
        zero_q = jnp.zeros((CHUNK, 2 * ATT_DH), BF16)

        def stacked_q(ln):
            q = q_ref[0, rows, ln]
            return jnp.concatenate([jnp.where(first, q, zero_q), jnp.where(first, zero_q, q)], axis=0)

        s = [_dot_nt(stacked_q(ln), kbuf_ref[band, ln]) for ln in lanes]
        s = [jnp.where(valid, x + bias_ref[pr], -jnp.inf) for x, pr in zip(s, pairs)]
        p = [jnp.exp2(x - jnp.max(x, axis=-1, keepdims=True)) for x in s]
        st["denom"] = [jnp.sum(x, axis=-1, keepdims=True) for x in p]
        st["p"] = [x.astype(BF16) for x in p]

    def values():
        pv = [_dot(x, vbuf_ref[band, ln]) for x, ln in zip(st["p"], lanes)]
        for x, dn, ln in zip(pv, st["denom"], lanes):
            x = x / dn
            ob_ref[rows, ln] = jnp.where(first, x[:CHUNK], x[CHUNK:]).astype(BF16)

    return [scores, values]


def _ffn_kernel(final_norm, x_ref, h_ref, oa_ref, q_ref, k_ref, v_ref, rel_ref, wg_ref, wa_ref,
                wb_ref, wo_ref, nfw_ref, wup_ref, cfw_ref, cfb_ref, wdn_ref, nlw_ref, out_ref,
                ubuf_ref, kbuf_ref, vbuf_ref, bias_ref, ob_ref):
    tm = x_ref.shape[1]
    d = x_ref.shape[2]
    dff = wdn_ref.shape[0]
    halo = V7X_SUBLANES
    x = x_ref[0]

    _attention_prepare(k_ref, v_ref, rel_ref, kbuf_ref, vbuf_ref, bias_ref)
    ubuf_ref[0:halo, :] = jnp.where(pl.program_id(1) == 0, 0.0, ubuf_ref[0:halo, :])
    attention = _Interleaver(
        [piece for c in range(tm // CHUNK)
         for piece in _attention_chunk(c, q_ref, kbuf_ref, vbuf_ref, bias_ref, ob_ref)], period=1)

    h = h_ref[0]
    gates = []
    for blk in range(2 * d // FFN_COLS):
        if blk % 2 == 0:
            attention()
        gates.append(jnp.dot(h, wg_ref[:, blk * FFN_COLS:(blk + 1) * FFN_COLS], preferred_element_type=F32))
        if blk % 2 == 1:
            attention()
    ya = jnp.dot(oa_ref[0], wa_ref[...], preferred_element_type=F32)
    attention.flush()
    gates = jnp.concatenate(gates, axis=1)
    yb = jnp.dot(ob_ref[...], wb_ref[...], preferred_element_type=F32)
    mix = _sigmoid(gates[:, :d]) * ya + _sigmoid(gates[:, d:]) * yb
    x1 = x + _dot(mix, wo_ref[...])

    h2 = (x1 * _rms_scale(x1) * nfw_ref[...]).astype(BF16)

    def gate_up_cols(blk):
        return [slice(half * dff + blk * FFN_COLS, half * dff + (blk + 1) * FFN_COLS) for half in range(2)]

    def project_up(blk):
        for cols in gate_up_cols(blk):
            ubuf_ref[halo:, cols] = jnp.dot(h2, wup_ref[:, cols], preferred_element_type=F32)

    def conv_block(cols):
        acc = cfb_ref[:, cols] + cfw_ref[FFN_CONV - 1:FFN_CONV, cols] * ubuf_ref[halo:halo + tm, cols]
        for tap in range(FFN_CONV - 1):
            off = halo - (FFN_CONV - 1) + tap
            acc = acc + cfw_ref[tap:tap + 1, cols] * ubuf_ref[off:off + tm, cols]
        ubuf_ref[0:halo, cols] = ubuf_ref[tm:tm + halo, cols]
        return acc

    n_blk = dff // FFN_COLS
    for blk in range(n_blk):
        project_up(blk)
    x2 = x1
    for blk in range(n_blk):
        gate, up = [conv_block(cols) for cols in gate_up_cols(blk)]
        act = (gate * _sigmoid(gate) * up).astype(BF16)
        x2 = x2 + jnp.dot(act, wdn_ref[blk * FFN_COLS:(blk + 1) * FFN_COLS, :], preferred_element_type=F32)
    if final_norm:
        x2 = x2 * _rms_scale(x2) * nlw_ref[...]
    out_ref[0] = x2


def _merge_ffn(x, h, oa, qkvb, rel, wg, wa, wb, wo, nfw, wup, cfw, cfb, wdn, nlw, final_norm):
    b, t, d = x.shape
    tm = FFN_ROWS
    dff = wdn.shape[0]
    consts = (rel, wg, wa, wb, wo, nfw, wup, cfw, cfb, wdn, nlw)
    return pl.pallas_call(
        functools.partial(_ffn_kernel, final_norm),
        grid=(b, t // tm),
        in_specs=[
            pl.BlockSpec((1, tm, d), lambda bi, i: (bi, i, 0)),
            pl.BlockSpec((1, tm, d), lambda bi, i: (bi, i, 0)),
            pl.BlockSpec((1, tm, VAL_A), lambda bi, i: (bi, i, 0)),
            pl.BlockSpec((1, tm, WIDTH_B), lambda bi, i: (bi, i, 0)),
            pl.BlockSpec((1, tm, WIDTH_B), lambda bi, i: (bi, i, 1)),
            pl.BlockSpec((1, tm, WIDTH_B), lambda bi, i: (bi, i, 2)),
        ] + [_const_spec(c.shape) for c in consts],
        out_specs=pl.BlockSpec((1, tm, d), lambda bi, i: (bi, i, 0)),
        out_shape=jax.ShapeDtypeStruct((b, t, d), F32),
        scratch_shapes=[
            pltpu.VMEM((tm + V7X_SUBLANES, 2 * dff), F32),
            pltpu.VMEM((LEAD + tm, WIDTH_B), BF16),
            pltpu.VMEM((LEAD + tm, WIDTH_B), BF16),
            pltpu.VMEM((ATT_HEADS // 2, 2 * CHUNK, BAND), F32),
            pltpu.VMEM((tm, WIDTH_B), BF16),
        ],
        compiler_params=_params(2),
        name="merge_ffn",
    )(x, h, oa, qkvb, qkvb, qkvb, *consts)


def _rel_bias_row(rel_table):
    n = jnp.arange(BIAS_ROLL)
    dist = jnp.where(n < BAND, jnp.clip(LEAD - n, -REL_CLIP, REL_CLIP), REL_CLIP)
    return rel_table.astype(F32)[:, dist + REL_CLIP]


def kernel(x, norm_mix_w, w_in, conv_qkv_w, a_log, dt_bias, gdn_norm_w, w_branch_a, w_branch_b,
           rel_bias, w_out, norm_ffn_w, w_up, conv_ffn_w, conv_ffn_b, w_down, norm_final_w):
    depth = w_in.shape[0]
    o_z = CONV_A + VAL_A
    o_bg = o_z + 2 * GDN_HEADS
    o_b = o_bg + 3 * WIDTH_B
    zeros_h = jnp.zeros((GDN_HEADS,), F32)
    for l in range(depth):
        w_l = w_in[l]

        def w_cols(lo, hi, w_l=w_l):
            return w_l[:, lo:hi].astype(BF16)

        head_params = jnp.stack([jnp.concatenate([zeros_h, a_log[l].astype(F32)]),
                                 jnp.concatenate([zeros_h, dt_bias[l].astype(F32)])], axis=1)
        a4, qkvb, gbc, gbr, h = _project(
            x, norm_mix_w[l][None, :], w_cols(0, o_z), w_cols(o_bg, o_b), w_cols(o_z, o_bg).T,
            conv_qkv_w[l], head_params)
        oa = _gated_delta(a4, gbc, gbr, gdn_norm_w[l][None, :])
        x = _merge_ffn(
            x, h, oa, qkvb, _rel_bias_row(rel_bias[l]), w_cols(o_b, w_l.shape[1]),
            w_branch_a[l].astype(BF16),
            w_branch_b[l].astype(BF16), w_out[l].astype(BF16), norm_ffn_w[l][None, :],
            w_up[l].astype(BF16), conv_ffn_w[l], conv_ffn_b[l][None, :], w_down[l].astype(BF16),
            norm_final_w[None, :], l == depth - 1)
    return x
```
